```python
import math
import functools
import jax
import jax.numpy as jnp
from jax import lax
import numpy as np

D_MODEL = 1024
BATCH = 8
SEQ = 8192
DEPTH = 1
DEC_BATCH = 128
DEC_SEQ = 4
PAST_LEN = 8192
PAGE_SIZE = 128

HEAD_DIM = 64
N_ATTN_HEADS = 8
ATTN_WIDTH = N_ATTN_HEADS * HEAD_DIM
CONV_WIDTH = D_MODEL - ATTN_WIDTH
CONV_KERNEL = 31
CONV_HIST = CONV_KERNEL - 1
IN_WIDTH = 3 * ATTN_WIDTH + 2 * CONV_WIDTH
DILATED_PATTERNS = ((128, 1), (512, 4), (2048, 16))
WINDOW_MAX = max(w for w, _ in DILATED_PATTERNS)
NUM_BUCKETS = 32
MAX_DISTANCE = WINDOW_MAX
D_FF = 4 * D_MODEL
N_MOD = 6
EPS = 1e-6

kernel_name = "hybrid_dilated_attn_conformer_conv_step"


def rms_norm(x, g):
    xf = x.astype(jnp.float32)
    y = xf * lax.rsqrt(jnp.mean(jnp.square(xf), axis=-1, keepdims=True) + EPS)
    return (y * g.astype(jnp.float32)).astype(x.dtype)


def layer_norm(x, g, b):
    xf = x.astype(jnp.float32)
    mu = jnp.mean(xf, axis=-1, keepdims=True)
    var = jnp.mean(jnp.square(xf - mu), axis=-1, keepdims=True)
    y = (xf - mu) * lax.rsqrt(var + EPS)
    return (y * g.astype(jnp.float32) + b.astype(jnp.float32)).astype(x.dtype)


def rel_bucket(dist):
    max_exact = NUM_BUCKETS // 2
    df = jnp.maximum(dist, 1).astype(jnp.float32)
    large = max_exact + (jnp.log(df / max_exact) / math.log(MAX_DISTANCE / max_exact)
                         * (NUM_BUCKETS - max_exact)).astype(jnp.int32)
    large = jnp.minimum(large, NUM_BUCKETS - 1)
    return jnp.where(dist < max_exact, dist, large)


def rel_bias_lookup(dist, rel_bias):
    return jnp.take(rel_bias, rel_bucket(dist), axis=0).astype(jnp.float32)


def dilated_prompt(q, k, v, rel_bias, window, dilation):
    B, S, H, Dh = q.shape
    nb = window // dilation
    Sp = -(-S // window) * window
    U = Sp // dilation
    nblk = U // nb

    def to_blocks(t):
        t = jnp.pad(t, ((0, 0), (0, Sp - S), (0, 0), (0, 0)))
        t = t.reshape(B, U, dilation, H, Dh).transpose(0, 2, 1, 3, 4)
        return t.reshape(B, dilation, nblk, nb, H, Dh)

    def with_prev(t):
        prev = jnp.concatenate([jnp.zeros_like(t[:, :, :1]), t[:, :, :-1]], axis=2)
        return jnp.concatenate([prev, t], axis=3)

    qb = to_blocks(q)
    kc = with_prev(to_blocks(k))
    vc = with_prev(to_blocks(v))
    i = jnp.arange(nb)[:, None]
    jj = jnp.arange(2 * nb)[None, :]
    delta = nb + i - jj
    blk = jnp.arange(nblk)[:, None, None]
    valid = (delta >= 0) & (delta <= nb) & (blk * nb + jj - nb >= 0)
    bias = rel_bias_lookup(jnp.maximum(delta, 0) * dilation, rel_bias).transpose(2, 0, 1)
    logits = jnp.einsum('brnqhd,brnkhd->brnhqk', qb, kc,
                        preferred_element_type=jnp.float32) * (HEAD_DIM ** -0.5) + bias
    logits = jnp.where(valid[None, None, :, None], logits, -jnp.inf)
    m = jnp.max(logits, axis=-1)
    e = jnp.exp(logits - m[..., None])
    l = jnp.sum(e, axis=-1)
    acc = jnp.einsum('brnhqk,brnkhd->brnqhd', e, vc.astype(jnp.float32))

    def stat_back(s):
        s = s.transpose(0, 1, 2, 4, 3).reshape(B, dilation, U, H).transpose(0, 2, 1, 3)
        return s.reshape(B, Sp, H)[:, :S]

    acc = acc.reshape(B, dilation, U, H, Dh).transpose(0, 2, 1, 3, 4).reshape(B, Sp, H, Dh)[:, :S]
    return stat_back(m), stat_back(l), acc


def dilated_cached(q, k_all, v_all, rel_bias, window, dilation, n_hist):
    T = q.shape[1]
    nb = window // dilation
    j = jnp.arange(nb + 1)
    idx = n_hist + jnp.arange(T)[:, None] - j[None, :] * dilation
    valid = idx >= 0
    idx_c = jnp.maximum(idx, 0)
    kg = k_all[:, idx_c]
    vg = v_all[:, idx_c]
    bias = rel_bias_lookup(j * dilation, rel_bias).T
    logits = jnp.einsum('bthd,btjhd->bthj', q, kg,
                        preferred_element_type=jnp.float32) * (HEAD_DIM ** -0.5) + bias
    logits = jnp.where(valid[None, :, None, :], logits, -jnp.inf)
    m = jnp.max(logits, axis=-1)
    e = jnp.exp(logits - m[..., None])
    l = jnp.sum(e, axis=-1)
    acc = jnp.einsum('bthj,btjhd->bthd', e, vg.astype(jnp.float32))
    return m, l, acc


def combine_dilations(parts):
    ms = jnp.stack([p[0] for p in parts])
    ls = jnp.stack([p[1] for p in parts])
    accs = jnp.stack([p[2] for p in parts])
    w = jnp.exp(ms - jnp.max(ms, axis=0, keepdims=True))
    return jnp.sum(w[..., None] * accs, axis=0) / jnp.sum(w * ls, axis=0)[..., None]


def attend_prompt(q, k, v, rel_bias):
    T = q.shape[1]
    parts = [dilated_prompt(q, k, v, rel_bias, w, d) for (w, d) in DILATED_PATTERNS]
    win = min(WINDOW_MAX, T)
    return combine_dilations(parts), k[:, T - win:], v[:, T - win:]


def attend_sample(q, k, v, cache_k_l, cache_v_l, rel_bias):
    n_hist = cache_k_l.shape[1]
    k_all = jnp.concatenate([cache_k_l, k.astype(cache_k_l.dtype)], axis=1)
    v_all = jnp.concatenate([cache_v_l, v.astype(cache_v_l.dtype)], axis=1)
    parts = [dilated_cached(q, k_all, v_all, rel_bias, w, d, n_hist) for (w, d) in DILATED_PATTERNS]
    return combine_dilations(parts), k_all[:, -n_hist:], v_all[:, -n_hist:]


def block(x, c, attend, conv_hist, norm1_g, norm2_g, w_ada, b_ada, w_in, q_norm_g, k_norm_g,
          conv_dw, conv_dw_b, conv_ln_g, conv_ln_b, w_pw2, w_o, w_ff1, w_ff2):
    B, T, _ = x.shape
    mod = (jax.nn.silu(c) @ w_ada + b_ada)[:, None, :]
    sh1, sc1, g1, sh2, sc2, g2 = jnp.split(mod, N_MOD, axis=-1)
    h = rms_norm(x, norm1_g) * (1 + sc1) + sh1
    z = h @ w_in
    q, k, v, u = jnp.split(z, [ATTN_WIDTH, 2 * ATTN_WIDTH, 3 * ATTN_WIDTH], axis=-1)
    q = rms_norm(q.reshape(B, T, N_ATTN_HEADS, HEAD_DIM), q_norm_g)
    k = rms_norm(k.reshape(B, T, N_ATTN_HEADS, HEAD_DIM), k_norm_g)
    v = v.reshape(B, T, N_ATTN_HEADS, HEAD_DIM)
    o_attn, new_k, new_v = attend(q, k, v)
    glu = u[..., :CONV_WIDTH] * jax.nn.sigmoid(u[..., CONV_WIDTH:])
    conv_in = jnp.concatenate([conv_hist.astype(glu.dtype), glu], axis=1)
    dw = lax.conv_general_dilated(conv_in, conv_dw[:, None, :], window_strides=(1,), padding='VALID',
                                  dimension_numbers=('NWC', 'WIO', 'NWC'),
                                  feature_group_count=CONV_WIDTH) + conv_dw_b
    o_conv = jax.nn.silu(layer_norm(dw, conv_ln_g, conv_ln_b)) @ w_pw2
    mix = jnp.concatenate([o_attn.reshape(B, T, ATTN_WIDTH).astype(x.dtype), o_conv], axis=-1) @ w_o
    x = x + g1 * mix
    h2 = rms_norm(x, norm2_g) * (1 + sc2) + sh2
    x = x + g2 * (jnp.square(jax.nn.relu(h2 @ w_ff1)) @ w_ff2)
    return x, new_k, new_v, conv_in[:, -CONV_HIST:]


def setup_inputs(seed: int = 0) -> dict:
    key = jax.random.key(seed)
    ks = jax.random.split(key, 24)
    f32 = jnp.float32
    winc = min(WINDOW_MAX, PAST_LEN)
    nrm = lambda k, shape, s: s * jax.random.normal(k, shape, f32)
    return {
        "x_prompt": nrm(ks[0], (BATCH, SEQ, D_MODEL), 1.0),
        "x_sample": nrm(ks[1], (DEC_BATCH, DEC_SEQ, D_MODEL), 1.0),
        "cache_k": nrm(ks[2], (DEPTH, DEC_BATCH, winc, N_ATTN_HEADS, HEAD_DIM), 1.0),
        "cache_v": nrm(ks[3], (DEPTH, DEC_BATCH, winc, N_ATTN_HEADS, HEAD_DIM), 1.0),
        "state_conv": nrm(ks[4], (DEPTH, DEC_BATCH, CONV_HIST, CONV_WIDTH), 0.5),
        "c_prompt": nrm(ks[5], (BATCH, D_MODEL), 1.0),
        "c_sample": nrm(ks[6], (DEC_BATCH, D_MODEL), 1.0),
        "rel_bias": nrm(ks[7], (NUM_BUCKETS, N_ATTN_HEADS), 0.5),
        "norm1_g": 1.0 + nrm(ks[8], (DEPTH, D_MODEL), 0.02),
        "norm2_g": 1.0 + nrm(ks[9], (DEPTH, D_MODEL), 0.02),
        "w_ada": nrm(ks[10], (DEPTH, D_MODEL, N_MOD * D_MODEL), 0.5 * D_MODEL ** -0.5),
        "b_ada": nrm(ks[11], (DEPTH, N_MOD * D_MODEL), 0.02),
        "w_in": nrm(ks[12], (DEPTH, D_MODEL, IN_WIDTH), D_MODEL ** -0.5),
        "q_norm_g": 1.0 + nrm(ks[13], (DEPTH, HEAD_DIM), 0.02),
        "k_norm_g": 1.0 + nrm(ks[14], (DEPTH, HEAD_DIM), 0.02),
        "conv_dw": nrm(ks[15], (DEPTH, CONV_KERNEL, CONV_WIDTH), CONV_KERNEL ** -0.5),
        "conv_dw_b": nrm(ks[16], (DEPTH, CONV_WIDTH), 0.02),
        "conv_ln_g": 1.0 + nrm(ks[17], (DEPTH, CONV_WIDTH), 0.02),
        "conv_ln_b": nrm(ks[18], (DEPTH, CONV_WIDTH), 0.02),
        "w_pw2": nrm(ks[19], (DEPTH, CONV_WIDTH, CONV_WIDTH), CONV_WIDTH ** -0.5),
        "w_o": nrm(ks[20], (DEPTH, D_MODEL, D_MODEL), D_MODEL ** -0.5),
        "w_ff1": nrm(ks[21], (DEPTH, D_MODEL, D_FF), D_MODEL ** -0.5),
        "w_ff2": nrm(ks[22], (DEPTH, D_FF, D_MODEL), D_FF ** -0.5),
    }


def reference(x_prompt, x_sample, cache_k, cache_v, state_conv, c_prompt, c_sample, rel_bias,
              norm1_g, norm2_g, w_ada, b_ada, w_in, q_norm_g, k_norm_g, conv_dw, conv_dw_b,
              conv_ln_g, conv_ln_b, w_pw2, w_o, w_ff1, w_ff2):
    xp, xs = x_prompt, x_sample
    kp_l, vp_l, cp_l, ks_l, vs_l, cs_l = [], [], [], [], [], []
    for l in range(DEPTH):
        lw = (norm1_g[l], norm2_g[l], w_ada[l], b_ada[l], w_in[l], q_norm_g[l], k_norm_g[l],
              conv_dw[l], conv_dw_b[l], conv_ln_g[l], conv_ln_b[l], w_pw2[l], w_o[l], w_ff1[l], w_ff2[l])
        hist_p = jnp.zeros((xp.shape[0], CONV_HIST, CONV_WIDTH), xp.dtype)
        xp, kp, vp, cp = block(xp, c_prompt, functools.partial(attend_prompt, rel_bias=rel_bias),
                               hist_p, *lw)
        att_s = functools.partial(attend_sample, cache_k_l=cache_k[l], cache_v_l=cache_v[l],
                                  rel_bias=rel_bias)
        xs, ks_, vs_, cs_ = block(xs, c_sample, att_s, state_conv[l], *lw)
        kp_l.append(kp); vp_l.append(vp); cp_l.append(cp)
        ks_l.append(ks_); vs_l.append(vs_); cs_l.append(cs_)
    return (xp, xs, jnp.stack(kp_l), jnp.stack(vp_l), jnp.stack(cp_l),
            jnp.stack(ks_l), jnp.stack(vs_l), jnp.stack(cs_l))
```

```python
import functools
import math

import jax
import jax.numpy as jnp
from jax import lax
from jax.experimental import pallas as pl
from jax.experimental.pallas import tpu as pltpu

F32 = jnp.float32
BF16 = jnp.bfloat16

HEAD_DIM = 64
N_HEADS = 8
ATTN_W = N_HEADS * HEAD_DIM
CONV_K = 31
CONV_HIST = CONV_K - 1
PATTERNS = ((128, 1), (512, 4), (2048, 16))
WINDOW_MAX = 2048
BAND = 128
NUM_BUCKETS = 32
EPS = 1e-6
N_MOD = 6
LANES = 128
HALO = 32
VMEM_LIMIT = 56 * 1024 * 1024

_NT = (((1,), (1,)), ((), ()))


def _dot(a, b):
    return jnp.dot(a, b, preferred_element_type=F32)


def _resident(shape):
    return pl.BlockSpec(shape, lambda *_: (0,) * len(shape), pipeline_mode=pl.Buffered(1))


def _params(sem):
    return pltpu.CompilerParams(dimension_semantics=sem, vmem_limit_bytes=VMEM_LIMIT)


def _mod_kernel(c_ref, w_ref, b_ref, o_ref):
    c = c_ref[...]
    a = c * jax.nn.sigmoid(c)
    w = w_ref[...]
    a_hi = a.astype(BF16)
    a_lo = (a - a_hi.astype(F32)).astype(BF16)
    w_hi = w.astype(BF16)
    w_lo = (w - w_hi.astype(F32)).astype(BF16)
    o_ref[...] = _dot(a_hi, w_hi) + _dot(a_lo, w_hi) + _dot(a_hi, w_lo) + b_ref[...]


def _modulation(c_all, w_ada, b_ada):
    rows, d = c_all.shape
    n = w_ada.shape[1]
    tn = d
    return pl.pallas_call(
        _mod_kernel,
        grid=(n // tn,),
        in_specs=[pl.BlockSpec((rows, d), lambda j: (0, 0)),
                  pl.BlockSpec((d, tn), lambda j: (0, j)),
                  pl.BlockSpec((1, tn), lambda j: (0, j))],
        out_specs=pl.BlockSpec((rows, tn), lambda j: (0, j)),
        out_shape=jax.ShapeDtypeStruct((rows, n), F32),
        compiler_params=_params(("arbitrary",)),
        name="mod",
    )(c_all, w_ada, b_ada.reshape(1, n))


def _rel_bucket(dist):
    max_exact = NUM_BUCKETS // 2
    df = jnp.maximum(dist, 1).astype(F32)
    large = max_exact + (jnp.log(df / max_exact) / math.log(WINDOW_MAX / max_exact)
                         * (NUM_BUCKETS - max_exact)).astype(jnp.int32)
    large = jnp.minimum(large, NUM_BUCKETS - 1)
    return jnp.where(dist < max_exact, dist, large)


def _prompt_bias_kernel(rel_ref, bkt_ref, o_ref):
    h = pl.program_id(1)
    bkt = bkt_ref[0]
    acc = jnp.full(bkt.shape, -jnp.inf, F32)
    for b in range(NUM_BUCKETS):
        acc = jnp.where(bkt == b, rel_ref[b, h], acc)
    o_ref[0, 0] = acc


def _prompt_bias_tables(rel_bias):
    i = jnp.arange(BAND)[:, None]
    jj = jnp.arange(2 * BAND)[None, :]
    delta = BAND + i - jj
    valid = (delta >= 0) & (delta <= BAND)
    bkt = jnp.stack([jnp.where(valid, _rel_bucket(jnp.maximum(delta, 0) * d), -1)
                     for (_, d) in PATTERNS]).astype(jnp.int32)
    npat = len(PATTERNS)
    return pl.pallas_call(
        _prompt_bias_kernel,
        grid=(npat, N_HEADS),
        in_specs=[pl.BlockSpec(memory_space=pltpu.SMEM),
                  pl.BlockSpec((1, BAND, 2 * BAND), lambda p, h: (p, 0, 0))],
        out_specs=pl.BlockSpec((1, 1, BAND, 2 * BAND), lambda p, h: (p, h, 0, 0)),
        out_shape=jax.ShapeDtypeStruct((npat, N_HEADS, BAND, 2 * BAND), F32),
        compiler_params=_params(("arbitrary", "arbitrary")),
        name="prompt_bias",
    )(rel_bias, bkt)


def _sample_bias_kernel(relrow_ref, bkt_ref, o_ref):
    bkt = bkt_ref[0]
    reps = bkt.shape[1] // LANES
    acc = jnp.full(bkt.shape, -jnp.inf, F32)
    for b in range(NUM_BUCKETS):
        acc = jnp.where(bkt == b, jnp.tile(relrow_ref[b], (1, reps)), acc)
    o_ref[0] = acc


def _sample_bias_tables(rel_bias, n_hist, t_new, n_keys):
    c = jnp.arange(n_keys)[None, :]
    t = jnp.arange(t_new)[:, None]
    dist = n_hist + t - c
    tabs = []
    for (w, d) in PATTERNS:
        valid = (dist >= 0) & (dist <= w) & (dist % d == 0) & (c < n_hist + t_new)
        tabs.append(jnp.where(valid, _rel_bucket(jnp.maximum(dist, 0)), -1))
    bkt = jnp.stack(tabs).astype(jnp.int32)
    bkt = jnp.repeat(bkt, N_HEADS, axis=1)
    relrow = jnp.broadcast_to(jnp.tile(rel_bias, (1, t_new))[:, :, None],
                              (NUM_BUCKETS, t_new * N_HEADS, LANES))
    npat = len(PATTERNS)
    rows = t_new * N_HEADS
    return pl.pallas_call(
        _sample_bias_kernel,
        grid=(npat,),
        in_specs=[pl.BlockSpec((NUM_BUCKETS, rows, LANES), lambda p: (0, 0, 0)),
                  pl.BlockSpec((1, rows, n_keys), lambda p: (p, 0, 0))],
        out_specs=pl.BlockSpec((1, rows, n_keys), lambda p: (p, 0, 0)),
        out_shape=jax.ShapeDtypeStruct((npat, rows, n_keys), F32),
        compiler_params=_params(("arbitrary",)),
        name="sample_bias",
    )(relrow, bkt)


def _modulated_rms(x, g, scale, shift):
    y = x * lax.rsqrt(jnp.mean(x * x, axis=-1, keepdims=True) + EPS) * g
    return y * (1.0 + scale) + shift


def _mod_slice(mod, k, d):
    return mod[:, k * d:(k + 1) * d]


def _head_norm(t, g, headmean_ref):
    ms = _dot((t * t).astype(BF16), headmean_ref[...])
    return t * lax.rsqrt(ms + EPS) * g


def _in_proj(hb, w_in_ref, qg_ref, kg_ref, headmean_ref):
    aw = ATTN_W
    q = _head_norm(_dot(hb, w_in_ref[:, 0:aw]), qg_ref[...], headmean_ref)
    k = _head_norm(_dot(hb, w_in_ref[:, aw:2 * aw]), kg_ref[...], headmean_ref)
    v = _dot(hb, w_in_ref[:, 2 * aw:3 * aw])
    cw = (w_in_ref.shape[1] - 3 * aw) // 2
    ua = _dot(hb, w_in_ref[:, 3 * aw:3 * aw + cw])
    ub = _dot(hb, w_in_ref[:, 3 * aw + cw:3 * aw + 2 * cw])
    return q, k, v, ua * jax.nn.sigmoid(ub)


def _conv_post(dw, lng_ref, lnb_ref):
    mu = jnp.mean(dw, axis=-1, keepdims=True)
    xc = dw - mu
    var = jnp.mean(xc * xc, axis=-1, keepdims=True)
    y = xc * lax.rsqrt(var + EPS) * lng_ref[...] + lnb_ref[...]
    return (y * jax.nn.sigmoid(y)).astype(BF16)


def _mlp_tail(x, mix, mod, n2g_ref, w1_ref, w2_ref, a_ref, ff_chunk):
    d = x.shape[-1]
    x1 = x + _mod_slice(mod, 2, d) * mix
    h2 = _modulated_rms(x1, n2g_ref[...], _mod_slice(mod, 4, d), _mod_slice(mod, 3, d)).astype(BF16)
    for c in range(w1_ref.shape[1] // ff_chunk):
        a = jnp.maximum(_dot(h2, w1_ref[:, c * ff_chunk:(c + 1) * ff_chunk]), 0.0)
        a_ref[:, c * ff_chunk:(c + 1) * ff_chunk] = (a * a).astype(BF16)
    return x1 + _mod_slice(mod, 5, d) * _dot(a_ref[...], w2_ref[...])


def _front_kernel(x_ref, mod_ref, n1g_ref, w_in_ref, qg_ref, kg_ref, headmean_ref,
                  cw_ref, cb_ref, lng_ref, lnb_ref, wpw_ref,
                  q_ref, k_ref, v_ref, oc_ref, kf_ref, vf_ref, nc_ref,
                  gbuf, act, *, tail_start, row_chunk):
    i = pl.program_id(1)
    n_i = pl.num_programs(1)
    tm = x_ref.shape[1]
    d = x_ref.shape[2]
    mod = mod_ref[0]
    h = _modulated_rms(x_ref[0], n1g_ref[...], _mod_slice(mod, 1, d), _mod_slice(mod, 0, d))
    q, k, v, glu = _in_proj(h.astype(BF16), w_in_ref, qg_ref, kg_ref, headmean_ref)
    q_ref[0] = (q * (HEAD_DIM ** -0.5)).astype(BF16)
    k_ref[0] = k.astype(BF16)
    v_ref[0] = v.astype(BF16)

    @pl.when(i >= tail_start)
    def _():
        kf_ref[0] = k
        vf_ref[0] = v

    @pl.when(i == 0)
    def _():
        gbuf[0:HALO, :] = jnp.zeros((HALO, gbuf.shape[1]), F32)

    gbuf[HALO:HALO + tm, :] = glu
    lead = HALO - CONV_HIST
    for c in range(tm // row_chunk):
        r0 = c * row_chunk + lead
        acc = jnp.broadcast_to(cb_ref[...], (row_chunk, gbuf.shape[1]))
        for j in range(CONV_K):
            acc = acc + gbuf[r0 + j:r0 + j + row_chunk, :] * cw_ref[j:j + 1, :]
        act[c * row_chunk:(c + 1) * row_chunk, :] = _conv_post(acc, lng_ref, lnb_ref)
    oc_ref[0] = _dot(act[...], wpw_ref[...]).astype(BF16)

    @pl.when(i == n_i - 1)
    def _():
        nc_ref[0] = gbuf[tm + lead:tm + HALO, :]

    gbuf[0:HALO, :] = gbuf[tm:tm + HALO, :]


def _prompt_front(x, mod, lw, tm=512, row_chunk=64):
    b, s, d = x.shape
    win = min(WINDOW_MAX, s)
    tail_start = (s - win) // tm
    cw = lw["conv_dw"].shape[1]
    tok = lambda shape: pl.BlockSpec((1, tm, shape), lambda bi, i: (bi, i, 0))
    tail = pl.BlockSpec((1, tm, ATTN_W), lambda bi, i: (bi, jnp.maximum(i - tail_start, 0), 0))
    in_specs = [tok(d),
                pl.BlockSpec((1, 1, N_MOD * d), lambda bi, i: (bi, 0, 0)),
                _resident((1, d)), _resident(lw["w_in"].shape), _resident((1, ATTN_W)),
                _resident((1, ATTN_W)), _resident((ATTN_W, ATTN_W)),
                _resident((CONV_K, cw)), _resident((1, cw)), _resident((1, cw)), _resident((1, cw)),
                _resident((cw, cw))]
    out_specs = [tok(ATTN_W), tok(ATTN_W), tok(ATTN_W), tok(cw), tail, tail,
                 pl.BlockSpec((1, CONV_HIST, cw), lambda bi, i: (bi, 0, 0))]
    out_shape = [jax.ShapeDtypeStruct((b, s, ATTN_W), BF16)] * 3 + [
        jax.ShapeDtypeStruct((b, s, cw), BF16),
        jax.ShapeDtypeStruct((b, win, ATTN_W), F32), jax.ShapeDtypeStruct((b, win, ATTN_W), F32),
        jax.ShapeDtypeStruct((b, CONV_HIST, cw), F32)]
    return pl.pallas_call(
        functools.partial(_front_kernel, tail_start=tail_start, row_chunk=row_chunk),
        grid=(b, s // tm),
        in_specs=in_specs, out_specs=out_specs, out_shape=out_shape,
        scratch_shapes=[pltpu.VMEM((tm + HALO, cw), F32), pltpu.VMEM((tm, cw), BF16)],
        compiler_params=_params(("arbitrary", "arbitrary")),
        name="prompt_front",
    )(x, mod.reshape(b, 1, N_MOD * d), lw["norm1_g"], lw["w_in"], lw["q_norm_g"], lw["k_norm_g"],
      lw["headmean"], lw["conv_dw"], lw["conv_dw_b"], lw["conv_ln_g"], lw["conv_ln_b"], lw["w_pw2"])


def _attn_kernel(q_ref, kp_ref, kc_ref, vp_ref, vc_ref, bias_ref, o_ref, st_ref, kbuf, vbuf):
    i = pl.program_id(2)
    qb = q_ref.shape[1]
    kbuf[0:BAND, :] = kp_ref[0]
    kbuf[BAND:BAND + qb, :] = kc_ref[0]
    vbuf[0:BAND, :] = vp_ref[0]
    vbuf[BAND:BAND + qb, :] = vc_ref[0]
    lane = lax.broadcasted_iota(jnp.int32, (BAND, LANES), 1)
    low = lane < HEAD_DIM
    col = lax.broadcasted_iota(jnp.int32, (1, 2 * BAND), 1)
    first_mask = jnp.where((col < BAND) & (i == 0), -jnp.inf, 0.0)
    zero = jnp.zeros((), BF16)
    for j in range(qb // BAND):
        rows = slice(j * BAND, (j + 1) * BAND)
        keys = slice(j * BAND, (j + 2) * BAND)
        stat = jnp.zeros((BAND, LANES), F32)
        for p in range(ATTN_W // LANES):
            cols = slice(p * LANES, (p + 1) * LANES)
            q2 = q_ref[0, rows, cols]
            k2 = kbuf[keys, cols]
            v2 = vbuf[keys, cols]
            halves = []
            for e in range(LANES // HEAD_DIM):
                hd = p * (LANES // HEAD_DIM) + e
                qm = jnp.where(low if e == 0 else jnp.logical_not(low), q2, zero)
                s = lax.dot_general(qm, k2, _NT, preferred_element_type=F32) + bias_ref[hd]
                if j == 0:
                    s = s + first_mask
                m = jnp.max(s, axis=-1, keepdims=True)
                ex = jnp.exp(s - m)
                l = jnp.sum(ex, axis=-1, keepdims=True)
                halves.append(_dot(ex.astype(BF16), v2) * (1.0 / l))
                stat = jnp.where(lane == hd, m + jnp.log(l), stat)
            o_ref[0, rows, cols] = jnp.where(low, halves[0], halves[1]).astype(BF16)
        st_ref[0, rows, :] = stat


def _prompt_attention(q, k, v, bias, dilation, qb=512):
    b, s, _ = q.shape
    u = s // dilation
    qb = min(qb, u)
    ratio = qb // BAND
    view = lambda t: t.reshape(b, u, dilation * ATTN_W)
    cur = pl.BlockSpec((1, qb, ATTN_W), lambda bi, r, i: (bi, i, r))
    prev = pl.BlockSpec((1, BAND, ATTN_W), lambda bi, r, i: (bi, jnp.maximum(i * ratio - 1, 0), r))
    o, st = pl.pallas_call(
        _attn_kernel,
        grid=(b, dilation, u // qb),
        in_specs=[cur, prev, cur, prev, cur, _resident(bias.shape)],
        out_specs=[cur, pl.BlockSpec((1, qb, LANES), lambda bi, r, i: (bi, i, r))],
        out_shape=[jax.ShapeDtypeStruct((b, u, dilation * ATTN_W), BF16),
                   jax.ShapeDtypeStruct((b, u, dilation * LANES), F32)],
        scratch_shapes=[pltpu.VMEM((BAND + qb, ATTN_W), BF16), pltpu.VMEM((BAND + qb, ATTN_W), BF16)],
        compiler_params=_params(("arbitrary", "arbitrary", "arbitrary")),
        name=f"prompt_attn_d{dilation}",
    )(view(q), view(k), view(k), view(v), view(v), bias)
    return o.reshape(b, s, ATTN_W), st.reshape(b, s, LANES)


def _back_kernel(x_ref, mod_ref, o1_ref, o2_ref, o3_ref, s1_ref, s2_ref, s3_ref, oc_ref,
                 expand_ref, wo_ref, n2g_ref, w1_ref, w2_ref, y_ref, a_ref, *, ff_chunk):
    mod = mod_ref[0]
    ls = [s1_ref[0], s2_ref[0], s3_ref[0]]
    top = jnp.maximum(jnp.maximum(ls[0], ls[1]), ls[2])
    ps = [jnp.exp(t - top) for t in ls]
    inv = 1.0 / (ps[0] + ps[1] + ps[2])
    o_attn = None
    for p_d, o_ref in zip(ps, (o1_ref, o2_ref, o3_ref)):
        w = p_d * inv
        w_hi = w.astype(BF16)
        w_lo = (w - w_hi.astype(F32)).astype(BF16)
        w_full = _dot(w_hi, expand_ref[...]) + _dot(w_lo, expand_ref[...])
        term = w_full * o_ref[0].astype(F32)
        o_attn = term if o_attn is None else o_attn + term
    aw = o_attn.shape[1]
    mix = _dot(o_attn.astype(BF16), wo_ref[0:aw, :]) + _dot(oc_ref[0], wo_ref[aw:, :])
    y_ref[0] = _mlp_tail(x_ref[0], mix, mod, n2g_ref, w1_ref, w2_ref, a_ref, ff_chunk)


def _prompt_back(x, mod, outs, stats, o_conv, lw, tm=512, ff_chunk=512):
    b, s, d = x.shape
    dff = lw["w_ff1"].shape[1]
    cw = o_conv.shape[2]
    tok = lambda width: pl.BlockSpec((1, tm, width), lambda bi, i: (bi, i, 0))
    in_specs = [tok(d), pl.BlockSpec((1, 1, N_MOD * d), lambda bi, i: (bi, 0, 0)),
                tok(ATTN_W), tok(ATTN_W), tok(ATTN_W), tok(LANES), tok(LANES), tok(LANES), tok(cw),
                _resident((LANES, ATTN_W)), _resident((d, d)), _resident((1, d)),
                _resident((d, dff)), _resident((dff, d))]
    return pl.pallas_call(
        functools.partial(_back_kernel, ff_chunk=ff_chunk),
        grid=(b, s // tm),
        in_specs=in_specs, out_specs=tok(d),
        out_shape=jax.ShapeDtypeStruct((b, s, d), F32),
        scratch_shapes=[pltpu.VMEM((tm, dff), BF16)],
        compiler_params=_params(("arbitrary", "arbitrary")),
        name="prompt_back",
    )(x, mod.reshape(b, 1, N_MOD * d), *outs, *stats, o_conv, lw["expand"], lw["w_o"],
      lw["norm2_g"], lw["w_ff1"], lw["w_ff2"])


def _sample_front_kernel(x_ref, mod_ref, hist_ref, n1g_ref, w_in_ref, qg_ref, kg_ref, headmean_ref,
                         cw_ref, cb_ref, lng_ref, lnb_ref, wpw_ref,
                         q_ref, k_ref, v_ref, glu_ref, oc_ref, hbuf, act, *, t_new):
    nb = mod_ref.shape[0]
    d = x_ref.shape[1]
    mod = mod_ref[...]
    for t in range(t_new):
        rows = slice(t * nb, (t + 1) * nb)
        h = _modulated_rms(x_ref[rows, :], n1g_ref[...], _mod_slice(mod, 1, d), _mod_slice(mod, 0, d))
        hbuf[rows, :] = h.astype(BF16)
    q, k, v, glu = _in_proj(hbuf[...], w_in_ref, qg_ref, kg_ref, headmean_ref)
    q_ref[...] = q
    k_ref[...] = k
    v_ref[...] = v
    glu_ref[...] = glu

    def slab(sidx):
        if sidx < CONV_HIST:
            return hist_ref[sidx]
        return glu_ref[(sidx - CONV_HIST) * nb:(sidx - CONV_HIST + 1) * nb, :]

    for t in range(t_new):
        acc = jnp.broadcast_to(cb_ref[...], (nb, glu.shape[1]))
        for j in range(CONV_K):
            acc = acc + slab(t + j) * cw_ref[j:j + 1, :]
        act[t * nb:(t + 1) * nb, :] = _conv_post(acc, lng_ref, lnb_ref)
    oc_ref[...] = _dot(act[...], wpw_ref[...]).astype(BF16)


def _sample_front(x_tm, mod, hist_tm, lw, t_new):
    rows, d = x_tm.shape
    nb = rows // t_new
    cw = lw["conv_dw"].shape[1]
    full = lambda shape: pl.BlockSpec(shape, lambda i: (0,) * len(shape))
    args = (x_tm, mod, hist_tm, lw["norm1_g"], lw["w_in"], lw["q_norm_g"], lw["k_norm_g"],
            lw["headmean"], lw["conv_dw"], lw["conv_dw_b"], lw["conv_ln_g"], lw["conv_ln_b"],
            lw["w_pw2"])
    return pl.pallas_call(
        functools.partial(_sample_front_kernel, t_new=t_new),
        grid=(1,),
        in_specs=[full(a.shape) for a in args],
        out_specs=[full((rows, ATTN_W))] * 3 + [full((rows, cw)), full((rows, cw))],
        out_shape=[jax.ShapeDtypeStruct((rows, ATTN_W), F32)] * 3 + [
            jax.ShapeDtypeStruct((rows, cw), F32), jax.ShapeDtypeStruct((rows, cw), BF16)],
        scratch_shapes=[pltpu.VMEM((rows, d), BF16), pltpu.VMEM((rows, cw), BF16)],
        compiler_params=_params(("arbitrary",)),
        name="sample_front",
    )(*args)


def _sample_attn_kernel(qbd_ref, kn_ref, vn_ref, ck_ref, cv_ref, tab_ref, hmask_ref,
                        nk_ref, nv_ref, o_ref, kb, vb, *, copy_chunk):
    n_hist = ck_ref.shape[1]
    t_new = kn_ref.shape[1]
    width = ck_ref.shape[2]
    n_keys = kb.shape[0]
    keep = n_hist - t_new
    for src_ref, new_ref, dst_ref, buf in ((ck_ref, kn_ref, nk_ref, kb), (cv_ref, vn_ref, nv_ref, vb)):
        for c in range(0, n_hist, copy_chunk):
            buf[c:c + copy_chunk, :] = src_ref[0, c:c + copy_chunk, :].astype(BF16)
            n = min(copy_chunk, keep - c)
            if n > 0:
                dst_ref[0, c:c + n, :] = src_ref[0, c + t_new:c + t_new + n, :]
        dst_ref[0, keep:n_hist, :] = new_ref[0]
        buf[n_hist:n_keys, :] = jnp.zeros((n_keys - n_hist, width), BF16)
        buf[n_hist:n_hist + t_new, :] = new_ref[0].astype(BF16)

    s = lax.dot_general(qbd_ref[0], kb[...], _NT, preferred_element_type=F32)
    ms, ls, accs = [], [], []
    for p in range(tab_ref.shape[0]):
        sp = s + tab_ref[p]
        m = jnp.max(sp, axis=-1, keepdims=True)
        ex = jnp.exp(sp - m)
        ms.append(m)
        ls.append(jnp.sum(ex, axis=-1, keepdims=True))
        accs.append(_dot(ex.astype(BF16), vb[...]))
    top = functools.reduce(jnp.maximum, ms)
    ws = [jnp.exp(m - top) for m in ms]
    num = sum(w * a for w, a in zip(ws, accs))
    den = sum(w * l for w, l in zip(ws, ls))
    rows = num * (1.0 / den) * hmask_ref[...]
    for t in range(t_new):
        o_ref[0, t:t + 1, :] = jnp.sum(rows[t * N_HEADS:(t + 1) * N_HEADS, :], axis=0, keepdims=True)


def _sample_attention(qbd, k_new, v_new, cache_k, cache_v, tabs, hmask, copy_chunk=256):
    nb, n_hist, width = cache_k.shape
    t_new = k_new.shape[1]
    n_keys = tabs.shape[2]
    per_b = lambda shape: pl.BlockSpec((1,) + shape, lambda bi: (bi, 0, 0))
    return pl.pallas_call(
        functools.partial(_sample_attn_kernel, copy_chunk=copy_chunk),
        grid=(nb,),
        in_specs=[per_b(qbd.shape[1:]), per_b((t_new, width)), per_b((t_new, width)),
                  per_b((n_hist, width)), per_b((n_hist, width)),
                  _resident(tabs.shape), _resident(hmask.shape)],
        out_specs=[per_b((n_hist, width)), per_b((n_hist, width)), per_b((t_new, width))],
        out_shape=[jax.ShapeDtypeStruct(cache_k.shape, cache_k.dtype),
                   jax.ShapeDtypeStruct(cache_v.shape, cache_v.dtype),
                   jax.ShapeDtypeStruct((nb, t_new, width), F32)],
        scratch_shapes=[pltpu.VMEM((n_keys, width), BF16), pltpu.VMEM((n_keys, width), BF16)],
        compiler_params=_params(("arbitrary",)),
        name="sample_attn",
    )(qbd, k_new, v_new, cache_k, cache_v, tabs, hmask)


def _sample_back_kernel(x_ref, mod_ref, oa_ref, oc_ref, wo_ref, n2g_ref, w1_ref, w2_ref,
                        y_ref, a_ref, *, ff_chunk):
    aw = oa_ref.shape[1]
    mix = _dot(oa_ref[...].astype(BF16), wo_ref[0:aw, :]) + _dot(oc_ref[...], wo_ref[aw:, :])
    y_ref[...] = _mlp_tail(x_ref[...], mix, mod_ref[...], n2g_ref, w1_ref, w2_ref, a_ref, ff_chunk)


def _sample_back(x_tm, mod, o_attn_tm, o_conv_tm, lw, t_new, ff_chunk=512):
    rows, d = x_tm.shape
    nb = rows // t_new
    dff = lw["w_ff1"].shape[1]
    cw = o_conv_tm.shape[1]
    tok = lambda width: pl.BlockSpec((nb, width), lambda t: (t, 0))
    return pl.pallas_call(
        functools.partial(_sample_back_kernel, ff_chunk=ff_chunk),
        grid=(t_new,),
        in_specs=[tok(d), _resident(mod.shape), tok(ATTN_W), tok(cw),
                  _resident((d, d)), _resident((1, d)), _resident((d, dff)), _resident((dff, d))],
        out_specs=tok(d),
        out_shape=jax.ShapeDtypeStruct((rows, d), F32),
        scratch_shapes=[pltpu.VMEM((nb, dff), BF16)],
        compiler_params=_params(("arbitrary",)),
        name="sample_back",
    )(x_tm, mod, o_attn_tm, o_conv_tm, lw["w_o"], lw["norm2_g"], lw["w_ff1"], lw["w_ff2"])


def _layer_weights(l, norm1_g, norm2_g, w_in, q_norm_g, k_norm_g, conv_dw, conv_dw_b,
                   conv_ln_g, conv_ln_b, w_pw2, w_o, w_ff1, w_ff2):
    head_of = jnp.arange(ATTN_W) // HEAD_DIM
    headmean = (head_of[:, None] == head_of[None, :]).astype(F32) / HEAD_DIM
    expand = (jnp.arange(LANES)[:, None] == head_of[None, :]).astype(BF16)
    row = lambda a: a[l].reshape(1, -1)
    return {
        "norm1_g": row(norm1_g), "norm2_g": row(norm2_g),
        "w_in": w_in[l].astype(BF16),
        "q_norm_g": jnp.tile(q_norm_g[l], N_HEADS).reshape(1, ATTN_W),
        "k_norm_g": jnp.tile(k_norm_g[l], N_HEADS).reshape(1, ATTN_W),
        "headmean": headmean.astype(BF16), "expand": expand,
        "conv_dw": conv_dw[l], "conv_dw_b": row(conv_dw_b),
        "conv_ln_g": row(conv_ln_g), "conv_ln_b": row(conv_ln_b),
        "w_pw2": w_pw2[l].astype(BF16), "w_o": w_o[l].astype(BF16),
        "w_ff1": w_ff1[l].astype(BF16), "w_ff2": w_ff2[l].astype(BF16),
    }


def kernel(x_prompt, x_sample, cache_k, cache_v, state_conv, c_prompt, c_sample, rel_bias, norm1_g, norm2_g, w_ada, b_ada, w_in, q_norm_g, k_norm_g, conv_dw, conv_dw_b, conv_ln_g, conv_ln_b, w_pw2, w_o, w_ff1, w_ff2):
    depth = w_in.shape[0]
    b, s, d = x_prompt.shape
    nb, t_new, _ = x_sample.shape
    n_hist = cache_k.shape[2]
    assert s % WINDOW_MAX == 0 and n_hist >= WINDOW_MAX and t_new <= PATTERNS[1][1]

    prompt_bias = _prompt_bias_tables(rel_bias)
    n_keys = -(-(n_hist + t_new) // LANES) * LANES
    sample_tabs = _sample_bias_tables(rel_bias, n_hist, t_new, n_keys)
    head_of = jnp.arange(ATTN_W) // HEAD_DIM
    hmask_h = (jnp.arange(N_HEADS)[:, None] == head_of[None, :]).astype(F32)
    hmask = jnp.tile(hmask_h, (t_new, 1))

    xp = x_prompt
    xs_tm = x_sample.transpose(1, 0, 2).reshape(t_new * nb, d)
    c_all = jnp.concatenate([c_prompt, c_sample], axis=0)
    kp_l, vp_l, cp_l, ks_l, vs_l, cs_l = [], [], [], [], [], []
    for l in range(depth):
        lw = _layer_weights(l, norm1_g, norm2_g, w_in, q_norm_g, k_norm_g, conv_dw, conv_dw_b,
                            conv_ln_g, conv_ln_b, w_pw2, w_o, w_ff1, w_ff2)
        mod = _modulation(c_all, w_ada[l], b_ada[l])
        mod_p, mod_s = mod[:b], mod[b:]

        q, k, v, o_conv, k_tail, v_tail, conv_tail = _prompt_front(xp, mod_p, lw)
        outs, stats = [], []
        for pi, (_, dil) in enumerate(PATTERNS):
            o_d, st_d = _prompt_attention(q, k, v, prompt_bias[pi], dil)
            outs.append(o_d)
            stats.append(st_d)
        xp = _prompt_back(xp, mod_p, outs, stats, o_conv, lw)
        kp_l.append(k_tail.reshape(b, -1, N_HEADS, HEAD_DIM))
        vp_l.append(v_tail.reshape(b, -1, N_HEADS, HEAD_DIM))
        cp_l.append(conv_tail)

        hist_tm = state_conv[l].transpose(1, 0, 2)
        q_s, k_s, v_s, glu_s, oc_s = _sample_front(xs_tm, mod_s, hist_tm, lw, t_new)
        to_bm = lambda a: a.reshape(t_new, nb, -1).transpose(1, 0, 2)
        q_bm = to_bm(q_s) * (HEAD_DIM ** -0.5)
        qbd = (q_bm[:, :, None, :] * hmask_h[None, None]).reshape(nb, t_new * N_HEADS, ATTN_W)
        new_k, new_v, o_s = _sample_attention(
            qbd.astype(BF16), to_bm(k_s), to_bm(v_s),
            cache_k[l].reshape(nb, n_hist, ATTN_W), cache_v[l].reshape(nb, n_hist, ATTN_W),
            sample_tabs, hmask)
        o_s_tm = o_s.transpose(1, 0, 2).reshape(t_new * nb, ATTN_W)
        xs_tm = _sample_back(xs_tm, mod_s, o_s_tm, oc_s, lw, t_new)
        ks_l.append(new_k.reshape(nb, n_hist, N_HEADS, HEAD_DIM))
        vs_l.append(new_v.reshape(nb, n_hist, N_HEADS, HEAD_DIM))
        cs_l.append(jnp.concatenate([state_conv[l].astype(F32), to_bm(glu_s)], axis=1)[:, -CONV_HIST:])

    xs = xs_tm.reshape(t_new, nb, d).transpose(1, 0, 2)
    return (xp, xs, jnp.stack(kp_l), jnp.stack(vp_l), jnp.stack(cp_l),
            jnp.stack(ks_l), jnp.stack(vs_l), jnp.stack(cs_l))
```

```python
import functools
import math

import jax
import jax.numpy as jnp
from jax import lax
from jax.experimental import pallas as pl
from jax.experimental.pallas import tpu as pltpu

F32 = jnp.float32
BF16 = jnp.bfloat16

HEAD_DIM = 64
N_HEADS = 8
ATTN_W = N_HEADS * HEAD_DIM
CONV_K = 31
CONV_HIST = CONV_K - 1
PATTERNS = ((128, 1), (512, 4), (2048, 16))
WINDOW_MAX = 2048
BAND = 128
NUM_BUCKETS = 32
EPS = 1e-6
N_MOD = 6
LANES = 128
SUBLANES = 8
HALO = 32
CONV_PITCH = 3
VMEM_LIMIT = 56 * 1024 * 1024
N_SLABS = ATTN_W // LANES

_NT = (((1,), (1,)), ((), ()))


def _dot(a, b):
    return jnp.dot(a, b, preferred_element_type=F32)


def _resident(shape):
    return pl.BlockSpec(shape, lambda *_: (0,) * len(shape), pipeline_mode=pl.Buffered(1))


def _params(sem):
    return pltpu.CompilerParams(dimension_semantics=sem, vmem_limit_bytes=VMEM_LIMIT)


def _mod_kernel(c_ref, w_ref, b_ref, o_ref):
    c = c_ref[...]
    a = c * jax.nn.sigmoid(c)
    w = w_ref[...]
    a_hi = a.astype(BF16)
    a_lo = (a - a_hi.astype(F32)).astype(BF16)
    w_hi = w.astype(BF16)
    w_lo = (w - w_hi.astype(F32)).astype(BF16)
    o_ref[...] = _dot(a_hi, w_hi) + _dot(a_lo, w_hi) + _dot(a_hi, w_lo) + b_ref[...]


def _modulation(c_all, w_ada, b_ada):
    rows, d = c_all.shape
    n = w_ada.shape[1]
    tn = d
    return pl.pallas_call(
        _mod_kernel,
        grid=(n // tn,),
        in_specs=[pl.BlockSpec((rows, d), lambda j: (0, 0)),
                  pl.BlockSpec((d, tn), lambda j: (0, j)),
                  pl.BlockSpec((1, tn), lambda j: (0, j))],
        out_specs=pl.BlockSpec((rows, tn), lambda j: (0, j)),
        out_shape=jax.ShapeDtypeStruct((rows, n), F32),
        compiler_params=_params(("arbitrary",)),
        name="mod",
    )(c_all, w_ada, b_ada.reshape(1, n))


def _rel_bucket(dist):
    max_exact = NUM_BUCKETS // 2
    df = jnp.maximum(dist, 1).astype(F32)
    large = max_exact + (jnp.log(df / max_exact) / math.log(WINDOW_MAX / max_exact)
                         * (NUM_BUCKETS - max_exact)).astype(jnp.int32)
    large = jnp.minimum(large, NUM_BUCKETS - 1)
    return jnp.where(dist < max_exact, dist, large)


def _prompt_bias_kernel(rel_ref, bkt_ref, o_ref):
    h = pl.program_id(1)
    bkt = bkt_ref[0]
    acc = jnp.full(bkt.shape, -jnp.inf, F32)
    for b in range(NUM_BUCKETS):
        acc = jnp.where(bkt == b, rel_ref[b, h], acc)
    o_ref[0, 0] = acc


def _prompt_bias_tables(rel_bias):
    i = jnp.arange(BAND)[:, None]
    jj = jnp.arange(2 * BAND)[None, :]
    delta = BAND + i - jj
    valid = (delta >= 0) & (delta <= BAND)
    bkt = jnp.stack([jnp.where(valid, _rel_bucket(jnp.maximum(delta, 0) * d), -1)
                     for (_, d) in PATTERNS]).astype(jnp.int32)
    npat = len(PATTERNS)
    return pl.pallas_call(
        _prompt_bias_kernel,
        grid=(npat, N_HEADS),
        in_specs=[pl.BlockSpec(memory_space=pltpu.SMEM),
                  pl.BlockSpec((1, BAND, 2 * BAND), lambda p, h: (p, 0, 0))],
        out_specs=pl.BlockSpec((1, 1, BAND, 2 * BAND), lambda p, h: (p, h, 0, 0)),
        out_shape=jax.ShapeDtypeStruct((npat, N_HEADS, BAND, 2 * BAND), F32),
        compiler_params=_params(("arbitrary", "arbitrary")),
        name="prompt_bias",
    )(rel_bias, bkt)


def _sample_bias_kernel(relrow_ref, bkt_ref, o_ref):
    bkt = bkt_ref[0]
    reps = bkt.shape[1] // LANES
    acc = jnp.full(bkt.shape, -jnp.inf, F32)
    for b in range(NUM_BUCKETS):
        acc = jnp.where(bkt == b, jnp.tile(relrow_ref[b], (1, reps)), acc)
    o_ref[0] = acc


def _sample_bias_tables(rel_bias, n_hist, t_new):
    n_keys = n_hist + LANES
    c = jnp.arange(n_keys)[None, :]
    t = jnp.arange(t_new)[:, None]
    key = jnp.where(c < n_hist, c, c - (LANES - t_new))
    real = (c < n_hist) | (c >= n_keys - t_new)
    dist = n_hist + t - key
    tabs = []
    for (w, d) in PATTERNS:
        valid = real & (dist >= 0) & (dist <= w) & (dist % d == 0)
        tabs.append(jnp.where(valid, _rel_bucket(jnp.maximum(dist, 0)), -1))
    bkt = jnp.stack(tabs).astype(jnp.int32)
    bkt = jnp.repeat(bkt, N_HEADS, axis=1)
    relrow = jnp.broadcast_to(jnp.tile(rel_bias, (1, t_new))[:, :, None],
                              (NUM_BUCKETS, t_new * N_HEADS, LANES))
    npat = len(PATTERNS)
    rows = t_new * N_HEADS
    return pl.pallas_call(
        _sample_bias_kernel,
        grid=(npat,),
        in_specs=[pl.BlockSpec((NUM_BUCKETS, rows, LANES), lambda p: (0, 0, 0)),
                  pl.BlockSpec((1, rows, n_keys), lambda p: (p, 0, 0))],
        out_specs=pl.BlockSpec((1, rows, n_keys), lambda p: (p, 0, 0)),
        out_shape=jax.ShapeDtypeStruct((npat, rows, n_keys), F32),
        compiler_params=_params(("arbitrary",)),
        name="sample_bias",
    )(relrow, bkt)


def _modulated_rms(x, g, scale, shift):
    y = x * lax.rsqrt(jnp.mean(x * x, axis=-1, keepdims=True) + EPS) * g
    return y * (1.0 + scale) + shift


def _mod_slice(mod, k, d):
    return mod[:, k * d:(k + 1) * d]


def _head_norm(t, g, headmean_ref):
    ms = _dot((t * t).astype(BF16), headmean_ref[...])
    return t * lax.rsqrt(ms + EPS) * g


def _in_proj(hb, w_in_ref, qg_ref, kg_ref, headmean_ref):
    aw = ATTN_W
    q = _head_norm(_dot(hb, w_in_ref[:, 0:aw]), qg_ref[...], headmean_ref)
    k = _head_norm(_dot(hb, w_in_ref[:, aw:2 * aw]), kg_ref[...], headmean_ref)
    v = _dot(hb, w_in_ref[:, 2 * aw:3 * aw])
    cw = (w_in_ref.shape[1] - 3 * aw) // 2
    ua = _dot(hb, w_in_ref[:, 3 * aw:3 * aw + cw])
    ub = _dot(hb, w_in_ref[:, 3 * aw + cw:3 * aw + 2 * cw])
    return q, k, v, ua * jax.nn.sigmoid(ub)


def _conv_post(dw, lng_ref, lnb_ref):
    mu = jnp.mean(dw, axis=-1, keepdims=True)
    xc = dw - mu
    var = jnp.mean(xc * xc, axis=-1, keepdims=True)
    y = xc * lax.rsqrt(var + EPS) * lng_ref[...] + lnb_ref[...]
    return (y * jax.nn.sigmoid(y)).astype(BF16)


def _mlp_tail(x, mix, mod, n2g_ref, w1_ref, w2_ref, a_ref, ff_chunk):
    d = x.shape[-1]
    x1 = x + _mod_slice(mod, 2, d) * mix
    h2 = _modulated_rms(x1, n2g_ref[...], _mod_slice(mod, 4, d), _mod_slice(mod, 3, d)).astype(BF16)
    for c in range(w1_ref.shape[1] // ff_chunk):
        a = jnp.maximum(_dot(h2, w1_ref[:, c * ff_chunk:(c + 1) * ff_chunk]), 0.0)
        a_ref[:, c * ff_chunk:(c + 1) * ff_chunk] = (a * a).astype(BF16)
    return x1 + _mod_slice(mod, 5, d) * _dot(a_ref[...], w2_ref[...])


def _slabs(x):
    return [x[:, s * LANES:(s + 1) * LANES] for s in range(x.shape[1] // LANES)]


def _regroup_store(x, nat, p4, out1_ref, out4_ref, out16_ref):
    tm = x.shape[0]
    d4 = PATTERNS[1][1]
    d16 = PATTERNS[2][1]
    sub = d16 // d4
    out1_ref[0] = x.astype(BF16)
    for s, xs in enumerate(_slabs(x)):
        nat[s] = xs
    for r4 in range(d4):
        parts = []
        for s in range(N_SLABS):
            part = nat[s, pl.ds(r4, tm // d4, stride=d4), :]
            p4[s, r4] = part
            parts.append(part)
        out4_ref[0, r4] = jnp.concatenate(parts, axis=1).astype(BF16)
    for r16 in range(d16):
        r4, a = r16 % d4, r16 // d4
        parts = [p4[s, r4, pl.ds(a, tm // d16, stride=sub), :] for s in range(N_SLABS)]
        out16_ref[0, r16] = jnp.concatenate(parts, axis=1).astype(BF16)


def _front_kernel(x_ref, mod_ref, n1g_ref, w_in_ref, qg_ref, kg_ref, headmean_ref,
                  cw_ref, cb_ref, lng_ref, lnb_ref, wpw_ref,
                  q1_ref, k1_ref, v1_ref, q4_ref, k4_ref, v4_ref, q16_ref, k16_ref, v16_ref,
                  oc_ref, kf_ref, vf_ref, nc_ref,
                  gbuf, act, nat, p4, *, tail_start, row_chunk):
    i = pl.program_id(1)
    n_i = pl.num_programs(1)
    tm = x_ref.shape[1]
    d = x_ref.shape[2]
    mod = mod_ref[0]
    h = _modulated_rms(x_ref[0], n1g_ref[...], _mod_slice(mod, 1, d), _mod_slice(mod, 0, d))
    q, k, v, glu = _in_proj(h.astype(BF16), w_in_ref, qg_ref, kg_ref, headmean_ref)

    @pl.when(i >= tail_start)
    def _():
        kf_ref[0] = k
        vf_ref[0] = v

    _regroup_store(q * (HEAD_DIM ** -0.5), nat, p4, q1_ref, q4_ref, q16_ref)
    _regroup_store(k, nat, p4, k1_ref, k4_ref, k16_ref)
    _regroup_store(v, nat, p4, v1_ref, v4_ref, v16_ref)

    def steps(first, count):
        return pl.ds(CONV_PITCH * first, count, stride=CONV_PITCH)

    n_cs = gbuf.shape[0]

    @pl.when(i == 0)
    def _():
        for s in range(n_cs):
            gbuf[s, steps(0, HALO), :] = jnp.zeros((HALO, LANES), F32)

    for s, part in enumerate(_slabs(glu)):
        gbuf[s, steps(HALO, tm), :] = part
    lead = HALO - CONV_HIST
    for c in range(tm // row_chunk):
        accs = [jnp.broadcast_to(b_s, (row_chunk, LANES)) for b_s in _slabs(cb_ref[...])]
        for j in range(CONV_K):
            for s in range(n_cs):
                accs[s] = accs[s] + (gbuf[s, steps(c * row_chunk + lead + j, row_chunk), :]
                                     * cw_ref[j:j + 1, s * LANES:(s + 1) * LANES])
        act[c * row_chunk:(c + 1) * row_chunk, :] = _conv_post(
            jnp.concatenate(accs, axis=1), lng_ref, lnb_ref)
    oc_ref[0] = _dot(act[...], wpw_ref[...]).astype(BF16)

    @pl.when(i == n_i - 1)
    def _():
        nc_ref[0] = jnp.concatenate(
            [gbuf[s, steps(tm + lead, CONV_HIST), :] for s in range(n_cs)], axis=1)

    for s in range(n_cs):
        gbuf[s, steps(0, HALO), :] = gbuf[s, steps(tm, HALO), :]


def _prompt_front(x, mod, lw, tm=512, row_chunk=64):
    b, s, d = x.shape
    win = min(WINDOW_MAX, s)
    tail_start = (s - win) // tm
    cw = lw["conv_dw"].shape[1]
    d4, d16 = PATTERNS[1][1], PATTERNS[2][1]
    tok = lambda width: pl.BlockSpec((1, tm, width), lambda bi, i: (bi, i, 0))
    plane = lambda dil: pl.BlockSpec((1, dil, tm // dil, ATTN_W), lambda bi, i: (bi, 0, i, 0))
    tail = pl.BlockSpec((1, tm, ATTN_W), lambda bi, i: (bi, jnp.maximum(i - tail_start, 0), 0))
    in_specs = [tok(d),
                pl.BlockSpec((1, 1, N_MOD * d), lambda bi, i: (bi, 0, 0)),
                _resident((1, d)), _resident(lw["w_in"].shape), _resident((1, ATTN_W)),
                _resident((1, ATTN_W)), _resident((ATTN_W, ATTN_W)),
                _resident((CONV_K, cw)), _resident((1, cw)), _resident((1, cw)), _resident((1, cw)),
                _resident((cw, cw))]
    out_specs = ([tok(ATTN_W)] * 3 + [plane(d4)] * 3 + [plane(d16)] * 3
                 + [tok(cw), tail, tail, pl.BlockSpec((1, CONV_HIST, cw), lambda bi, i: (bi, 0, 0))])
    nat_shape = jax.ShapeDtypeStruct((b, s, ATTN_W), BF16)
    plane_shape = lambda dil: jax.ShapeDtypeStruct((b, dil, s // dil, ATTN_W), BF16)
    out_shape = ([nat_shape] * 3 + [plane_shape(d4)] * 3 + [plane_shape(d16)] * 3 + [
        jax.ShapeDtypeStruct((b, s, cw), BF16),
        jax.ShapeDtypeStruct((b, win, ATTN_W), F32), jax.ShapeDtypeStruct((b, win, ATTN_W), F32),
        jax.ShapeDtypeStruct((b, CONV_HIST, cw), F32)])
    return pl.pallas_call(
        functools.partial(_front_kernel, tail_start=tail_start, row_chunk=row_chunk),
        grid=(b, s // tm),
        in_specs=in_specs, out_specs=out_specs, out_shape=out_shape,
        scratch_shapes=[pltpu.VMEM((cw // LANES, CONV_PITCH * (tm + HALO), LANES), F32),
                        pltpu.VMEM((tm, cw), BF16),
                        pltpu.VMEM((N_SLABS, tm, LANES), F32),
                        pltpu.VMEM((N_SLABS, d4, tm // d4, LANES), F32)],
        compiler_params=_params(("arbitrary", "arbitrary")),
        name="prompt_front",
    )(x, mod.reshape(b, 1, N_MOD * d), lw["norm1_g"], lw["w_in"], lw["q_norm_g"], lw["k_norm_g"],
      lw["headmean"], lw["conv_dw"], lw["conv_dw_b"], lw["conv_ln_g"], lw["conv_ln_b"], lw["w_pw2"])


def _attn_kernel(q_ref, kp_ref, kc_ref, vp_ref, vc_ref, bias_ref, o_ref, st_ref, kbuf, vbuf):
    i = pl.program_id(2)
    qb = q_ref.shape[2]
    kbuf[0:BAND, :] = kp_ref[0, 0]
    kbuf[BAND:BAND + qb, :] = kc_ref[0, 0]
    vbuf[0:BAND, :] = vp_ref[0, 0]
    vbuf[BAND:BAND + qb, :] = vc_ref[0, 0]
    lane = lax.broadcasted_iota(jnp.int32, (BAND, LANES), 1)
    low = lane < HEAD_DIM
    col = lax.broadcasted_iota(jnp.int32, (1, 2 * BAND), 1)
    first_mask = jnp.where((col < BAND) & (i == 0), -jnp.inf, 0.0)
    zero = jnp.zeros((), BF16)
    for j in range(qb // BAND):
        rows = slice(j * BAND, (j + 1) * BAND)
        keys = slice(j * BAND, (j + 2) * BAND)
        stat = jnp.zeros((BAND, LANES), F32)
        for p in range(N_SLABS):
            cols = slice(p * LANES, (p + 1) * LANES)
            q2 = q_ref[0, 0, rows, cols]
            k2 = kbuf[keys, cols]
            v2 = vbuf[keys, cols]
            halves = []
            for e in range(LANES // HEAD_DIM):
                hd = p * (LANES // HEAD_DIM) + e
                qm = jnp.where(low if e == 0 else jnp.logical_not(low), q2, zero)
                s = lax.dot_general(qm, k2, _NT, preferred_element_type=F32) + bias_ref[hd]
                if j == 0:
                    s = s + first_mask
                m = jnp.max(s, axis=-1, keepdims=True)
                ex = jnp.exp(s - m)
                l = jnp.sum(ex, axis=-1, keepdims=True)
                halves.append(_dot(ex.astype(BF16), v2) * (1.0 / l))
                stat = jnp.where(lane == hd, m + jnp.log(l), stat)
            o_ref[0, 0, rows, cols] = jnp.where(low, halves[0], halves[1]).astype(BF16)
        st_ref[0, 0, rows, :] = stat


def _prompt_attention(q, k, v, bias, qb=512):
    b, dil, u, _ = q.shape
    qb = min(qb, u)
    ratio = qb // BAND
    cur = pl.BlockSpec((1, 1, qb, ATTN_W), lambda bi, r, i: (bi, r, i, 0))
    prev = pl.BlockSpec((1, 1, BAND, ATTN_W),
                        lambda bi, r, i: (bi, r, jnp.maximum(i * ratio - 1, 0), 0))
    return pl.pallas_call(
        _attn_kernel,
        grid=(b, dil, u // qb),
        in_specs=[cur, prev, cur, prev, cur, _resident(bias.shape)],
        out_specs=[cur, pl.BlockSpec((1, 1, qb, LANES), lambda bi, r, i: (bi, r, i, 0))],
        out_shape=[jax.ShapeDtypeStruct((b, dil, u, ATTN_W), BF16),
                   jax.ShapeDtypeStruct((b, dil, u, LANES), F32)],
        scratch_shapes=[pltpu.VMEM((BAND + qb, ATTN_W), BF16), pltpu.VMEM((BAND + qb, ATTN_W), BF16)],
        compiler_params=_params(("arbitrary", "arbitrary", "arbitrary")),
        name=f"prompt_attn_d{dil}",
    )(q, k, k, v, v, bias)


def _back_kernel(x_ref, mod_ref, o1_ref, o4_ref, o16_ref, s1_ref, s4_ref, s16_ref, oc_ref,
                 expand_ref, wo_ref, n2g_ref, w1_ref, w2_ref, y_ref, a_ref, lnat, onat, *, ff_chunk):
    mod = mod_ref[0]
    tm = x_ref.shape[1]
    for slot, (o_ref, s_ref) in enumerate(((o4_ref, s4_ref), (o16_ref, s16_ref))):
        dil = o_ref.shape[1]
        for r in range(dil):
            rows = pl.ds(r, tm // dil, stride=dil)
            lnat[slot, rows, :] = s_ref[0, r]
            for s, part in enumerate(_slabs(o_ref[0, r])):
                onat[slot, s, rows, :] = part.astype(F32)
    ls = [s1_ref[0, 0], lnat[0], lnat[1]]
    top = jnp.maximum(jnp.maximum(ls[0], ls[1]), ls[2])
    ps = [jnp.exp(t - top) for t in ls]
    inv = 1.0 / (ps[0] + ps[1] + ps[2])
    os = [o1_ref[0, 0].astype(F32)] + [
        jnp.concatenate([onat[slot, s] for s in range(N_SLABS)], axis=1) for slot in range(2)]
    o_attn = None
    for p_d, o_d in zip(ps, os):
        w = p_d * inv
        w_hi = w.astype(BF16)
        w_lo = (w - w_hi.astype(F32)).astype(BF16)
        w_full = _dot(w_hi, expand_ref[...]) + _dot(w_lo, expand_ref[...])
        term = w_full * o_d
        o_attn = term if o_attn is None else o_attn + term
    aw = o_attn.shape[1]
    mix = _dot(o_attn.astype(BF16), wo_ref[0:aw, :]) + _dot(oc_ref[0], wo_ref[aw:, :])
    y_ref[0] = _mlp_tail(x_ref[0], mix, mod, n2g_ref, w1_ref, w2_ref, a_ref, ff_chunk)


def _prompt_back(x, mod, outs, stats, o_conv, lw, tm=512, ff_chunk=512):
    b, s, d = x.shape
    dff = lw["w_ff1"].shape[1]
    cw = o_conv.shape[2]
    tok = lambda width: pl.BlockSpec((1, tm, width), lambda bi, i: (bi, i, 0))
    plane = lambda a: pl.BlockSpec((1, a.shape[1], tm // a.shape[1], a.shape[3]),
                                   lambda bi, i: (bi, 0, i, 0))
    in_specs = ([tok(d), pl.BlockSpec((1, 1, N_MOD * d), lambda bi, i: (bi, 0, 0))]
                + [plane(a) for a in outs] + [plane(a) for a in stats]
                + [tok(cw), _resident((LANES, ATTN_W)), _resident((d, d)), _resident((1, d)),
                   _resident((d, dff)), _resident((dff, d))])
    return pl.pallas_call(
        functools.partial(_back_kernel, ff_chunk=ff_chunk),
        grid=(b, s // tm),
        in_specs=in_specs, out_specs=tok(d),
        out_shape=jax.ShapeDtypeStruct((b, s, d), F32),
        scratch_shapes=[pltpu.VMEM((tm, dff), BF16), pltpu.VMEM((2, tm, LANES), F32),
                        pltpu.VMEM((2, N_SLABS, tm, LANES), F32)],
        compiler_params=_params(("arbitrary", "arbitrary")),
        name="prompt_back",
    )(x, mod.reshape(b, 1, N_MOD * d), *outs, *stats, o_conv, lw["expand"], lw["w_o"],
      lw["norm2_g"], lw["w_ff1"], lw["w_ff2"])


def _sample_front_kernel(x_ref, mod_ref, hist_ref, n1g_ref, w_in_ref, qg_ref, kg_ref, headmean_ref,
                         cw_ref, cb_ref, lng_ref, lnb_ref, wpw_ref,
                         q_ref, k_ref, v_ref, glu_ref, oc_ref, hbuf, act, *, t_new):
    nb = mod_ref.shape[0]
    d = x_ref.shape[1]
    mod = mod_ref[...]
    for t in range(t_new):
        rows = slice(t * nb, (t + 1) * nb)
        h = _modulated_rms(x_ref[rows, :], n1g_ref[...], _mod_slice(mod, 1, d), _mod_slice(mod, 0, d))
        hbuf[rows, :] = h.astype(BF16)
    q, k, v, glu = _in_proj(hbuf[...], w_in_ref, qg_ref, kg_ref, headmean_ref)
    q_ref[...] = q
    k_ref[...] = k
    v_ref[...] = v
    glu_ref[...] = glu

    def slab(sidx):
        if sidx < CONV_HIST:
            return hist_ref[sidx]
        return glu_ref[(sidx - CONV_HIST) * nb:(sidx - CONV_HIST + 1) * nb, :]

    for t in range(t_new):
        acc = jnp.broadcast_to(cb_ref[...], (nb, glu.shape[1]))
        for j in range(CONV_K):
            acc = acc + slab(t + j) * cw_ref[j:j + 1, :]
        act[t * nb:(t + 1) * nb, :] = _conv_post(acc, lng_ref, lnb_ref)
    oc_ref[...] = _dot(act[...], wpw_ref[...]).astype(BF16)


def _sample_front(x_tm, mod, hist_tm, lw, t_new):
    rows, d = x_tm.shape
    cw = lw["conv_dw"].shape[1]
    full = lambda shape: pl.BlockSpec(shape, lambda i: (0,) * len(shape))
    args = (x_tm, mod, hist_tm, lw["norm1_g"], lw["w_in"], lw["q_norm_g"], lw["k_norm_g"],
            lw["headmean"], lw["conv_dw"], lw["conv_dw_b"], lw["conv_ln_g"], lw["conv_ln_b"],
            lw["w_pw2"])
    return pl.pallas_call(
        functools.partial(_sample_front_kernel, t_new=t_new),
        grid=(1,),
        in_specs=[full(a.shape) for a in args],
        out_specs=[full((rows, ATTN_W))] * 3 + [full((rows, cw)), full((rows, cw))],
        out_shape=[jax.ShapeDtypeStruct((rows, ATTN_W), F32)] * 3 + [
            jax.ShapeDtypeStruct((rows, cw), F32), jax.ShapeDtypeStruct((rows, cw), BF16)],
        scratch_shapes=[pltpu.VMEM((rows, d), BF16), pltpu.VMEM((rows, cw), BF16)],
        compiler_params=_params(("arbitrary",)),
        name="sample_front",
    )(*args)


def _sample_attn_kernel(qbd_ref, kn_ref, vn_ref, ck_ref, cv_ref, tab_ref, hmask_ref,
                        nk_ref, nv_ref, o_ref, kb, vb, *, t_new, row_chunk):
    width, n_hist = ck_ref.shape[1], ck_ref.shape[2]
    keep = n_hist - LANES
    lane = lax.broadcasted_iota(jnp.int32, (row_chunk, LANES), 1)
    is_new = lane >= LANES - t_new
    for src_ref, new_ref, dst_ref, buf in ((ck_ref, kn_ref, nk_ref, kb), (cv_ref, vn_ref, nv_ref, vb)):
        for c in range(0, width, row_chunk):
            rows = slice(c, c + row_chunk)
            old = src_ref[0, rows, :]
            new = new_ref[0, rows, :]
            buf[rows, 0:n_hist] = old.astype(BF16)
            buf[rows, n_hist:] = new.astype(BF16)
            shifted = pltpu.roll(old, n_hist - t_new, axis=1)
            dst_ref[0, rows, 0:keep] = shifted[:, 0:keep]
            dst_ref[0, rows, keep:] = jnp.where(is_new, new, shifted[:, keep:])

    s = _dot(qbd_ref[0], kb[...])
    ms, ls, exs = [], [], []
    for p in range(tab_ref.shape[0]):
        sp = s + tab_ref[p]
        m = jnp.max(sp, axis=-1, keepdims=True)
        ex = jnp.exp(sp - m)
        ms.append(m)
        ls.append(jnp.sum(ex, axis=-1, keepdims=True))
        exs.append(ex.astype(BF16))
    pv = lax.dot_general(jnp.concatenate(exs, axis=0), vb[...], _NT, preferred_element_type=F32)
    nr = s.shape[0]
    accs = [pv[p * nr:(p + 1) * nr, :] for p in range(len(exs))]
    top = functools.reduce(jnp.maximum, ms)
    ws = [jnp.exp(m - top) for m in ms]
    num = sum(w * a for w, a in zip(ws, accs))
    den = sum(w * l for w, l in zip(ws, ls))
    rows = num * (1.0 / den) * hmask_ref[...]
    for t in range(t_new):
        o_ref[0, t:t + 1, :] = jnp.sum(rows[t * N_HEADS:(t + 1) * N_HEADS, :], axis=0, keepdims=True)


def _sample_attention(qbd, k_new_t, v_new_t, cache_k_t, cache_v_t, tabs, hmask, t_new, row_chunk=64):
    nb, width, n_hist = cache_k_t.shape
    n_keys = tabs.shape[2]
    per_b = lambda shape: pl.BlockSpec((1,) + shape, lambda bi: (bi, 0, 0))
    return pl.pallas_call(
        functools.partial(_sample_attn_kernel, t_new=t_new, row_chunk=row_chunk),
        grid=(nb,),
        in_specs=[per_b(qbd.shape[1:]), per_b((width, LANES)), per_b((width, LANES)),
                  per_b((width, n_hist)), per_b((width, n_hist)),
                  _resident(tabs.shape), _resident(hmask.shape)],
        out_specs=[per_b((width, n_hist)), per_b((width, n_hist)), per_b((t_new, width))],
        out_shape=[jax.ShapeDtypeStruct(cache_k_t.shape, cache_k_t.dtype),
                   jax.ShapeDtypeStruct(cache_v_t.shape, cache_v_t.dtype),
                   jax.ShapeDtypeStruct((nb, t_new, width), F32)],
        scratch_shapes=[pltpu.VMEM((width, n_keys), BF16), pltpu.VMEM((width, n_keys), BF16)],
        compiler_params=_params(("arbitrary",)),
        name="sample_attn",
    )(qbd, k_new_t, v_new_t, cache_k_t, cache_v_t, tabs, hmask)


def _sample_back_kernel(x_ref, mod_ref, oa_ref, oc_ref, wo_ref, n2g_ref, w1_ref, w2_ref,
                        y_ref, a_ref, *, ff_chunk):
    aw = oa_ref.shape[1]
    mix = _dot(oa_ref[...].astype(BF16), wo_ref[0:aw, :]) + _dot(oc_ref[...], wo_ref[aw:, :])
    y_ref[...] = _mlp_tail(x_ref[...], mix, mod_ref[...], n2g_ref, w1_ref, w2_ref, a_ref, ff_chunk)


def _sample_back(x_tm, mod, o_attn_tm, o_conv_tm, lw, t_new, ff_chunk=512):
    rows, d = x_tm.shape
    nb = rows // t_new
    dff = lw["w_ff1"].shape[1]
    cw = o_conv_tm.shape[1]
    tok = lambda width: pl.BlockSpec((nb, width), lambda t: (t, 0))
    return pl.pallas_call(
        functools.partial(_sample_back_kernel, ff_chunk=ff_chunk),
        grid=(t_new,),
        in_specs=[tok(d), _resident(mod.shape), tok(ATTN_W), tok(cw),
                  _resident((d, d)), _resident((1, d)), _resident((d, dff)), _resident((dff, d))],
        out_specs=tok(d),
        out_shape=jax.ShapeDtypeStruct((rows, d), F32),
        scratch_shapes=[pltpu.VMEM((nb, dff), BF16)],
        compiler_params=_params(("arbitrary",)),
        name="sample_back",
    )(x_tm, mod, o_attn_tm, o_conv_tm, lw["w_o"], lw["norm2_g"], lw["w_ff1"], lw["w_ff2"])


def _layer_weights(l, norm1_g, norm2_g, w_in, q_norm_g, k_norm_g, conv_dw, conv_dw_b,
                   conv_ln_g, conv_ln_b, w_pw2, w_o, w_ff1, w_ff2):
    head_of = jnp.arange(ATTN_W) // HEAD_DIM
    headmean = (head_of[:, None] == head_of[None, :]).astype(F32) / HEAD_DIM
    expand = (jnp.arange(LANES)[:, None] == head_of[None, :]).astype(BF16)
    row = lambda a: a[l].reshape(1, -1)
    return {
        "norm1_g": row(norm1_g), "norm2_g": row(norm2_g),
        "w_in": w_in[l].astype(BF16),
        "q_norm_g": jnp.tile(q_norm_g[l], N_HEADS).reshape(1, ATTN_W),
        "k_norm_g": jnp.tile(k_norm_g[l], N_HEADS).reshape(1, ATTN_W),
        "headmean": headmean.astype(BF16), "expand": expand,
        "conv_dw": conv_dw[l], "conv_dw_b": row(conv_dw_b),
        "conv_ln_g": row(conv_ln_g), "conv_ln_b": row(conv_ln_b),
        "w_pw2": w_pw2[l].astype(BF16), "w_o": w_o[l].astype(BF16),
        "w_ff1": w_ff1[l].astype(BF16), "w_ff2": w_ff2[l].astype(BF16),
    }


def _feature_major(cache):
    nb, n_hist = cache.shape[0], cache.shape[1]
    return cache.transpose(0, 2, 3, 1).reshape(nb, ATTN_W, n_hist)


def _position_major(cache_t):
    nb, _, n_hist = cache_t.shape
    return cache_t.reshape(nb, N_HEADS, HEAD_DIM, n_hist).transpose(0, 3, 1, 2)


def kernel(x_prompt, x_sample, cache_k, cache_v, state_conv, c_prompt, c_sample, rel_bias, norm1_g, norm2_g, w_ada, b_ada, w_in, q_norm_g, k_norm_g, conv_dw, conv_dw_b, conv_ln_g, conv_ln_b, w_pw2, w_o, w_ff1, w_ff2):
    depth = w_in.shape[0]
    b, s, d = x_prompt.shape
    nb, t_new, _ = x_sample.shape
    n_hist = cache_k.shape[2]
    assert s % WINDOW_MAX == 0 and n_hist % LANES == 0 and t_new <= LANES

    prompt_bias = _prompt_bias_tables(rel_bias)
    sample_tabs = _sample_bias_tables(rel_bias, n_hist, t_new)
    head_of = jnp.arange(ATTN_W) // HEAD_DIM
    hmask_h = (jnp.arange(N_HEADS)[:, None] == head_of[None, :]).astype(F32)
    hmask = jnp.tile(hmask_h, (t_new, 1))

    xp = x_prompt
    xs_tm = x_sample.transpose(1, 0, 2).reshape(t_new * nb, d)
    c_all = jnp.concatenate([c_prompt, c_sample], axis=0)
    kp_l, vp_l, cp_l, ks_l, vs_l, cs_l = [], [], [], [], [], []
    for l in range(depth):
        lw = _layer_weights(l, norm1_g, norm2_g, w_in, q_norm_g, k_norm_g, conv_dw, conv_dw_b,
                            conv_ln_g, conv_ln_b, w_pw2, w_o, w_ff1, w_ff2)
        mod = _modulation(c_all, w_ada[l], b_ada[l])
        mod_p, mod_s = mod[:b], mod[b:]

        (q1, k1, v1, q4, k4, v4, q16, k16, v16,
         o_conv, k_tail, v_tail, conv_tail) = _prompt_front(xp, mod_p, lw)
        unit = lambda a: a.reshape(b, 1, s, ATTN_W)
        outs, stats = [], []
        for pi, qkv in enumerate(((unit(q1), unit(k1), unit(v1)), (q4, k4, v4), (q16, k16, v16))):
            o_d, st_d = _prompt_attention(*qkv, prompt_bias[pi])
            outs.append(o_d)
            stats.append(st_d)
        xp = _prompt_back(xp, mod_p, outs, stats, o_conv, lw)
        kp_l.append(k_tail.reshape(b, -1, N_HEADS, HEAD_DIM))
        vp_l.append(v_tail.reshape(b, -1, N_HEADS, HEAD_DIM))
        cp_l.append(conv_tail)

        hist_tm = state_conv[l].transpose(1, 0, 2)
        q_s, k_s, v_s, glu_s, oc_s = _sample_front(xs_tm, mod_s, hist_tm, lw, t_new)
        to_bm = lambda a: a.reshape(t_new, nb, -1).transpose(1, 0, 2)
        q_bm = to_bm(q_s) * (HEAD_DIM ** -0.5)
        qbd = (q_bm[:, :, None, :] * hmask_h[None, None]).reshape(nb, t_new * N_HEADS, ATTN_W)
        new_t = lambda a: jnp.pad(a.reshape(t_new, nb, ATTN_W).transpose(1, 2, 0),
                                  ((0, 0), (0, 0), (LANES - t_new, 0)))
        new_k_t, new_v_t, o_s = _sample_attention(
            qbd.astype(BF16), new_t(k_s), new_t(v_s),
            _feature_major(cache_k[l]), _feature_major(cache_v[l]), sample_tabs, hmask, t_new)
        o_s_tm = o_s.transpose(1, 0, 2).reshape(t_new * nb, ATTN_W)
        xs_tm = _sample_back(xs_tm, mod_s, o_s_tm, oc_s, lw, t_new)
        ks_l.append(_position_major(new_k_t))
        vs_l.append(_position_major(new_v_t))
        cs_l.append(jnp.concatenate([state_conv[l].astype(F32), to_bm(glu_s)], axis=1)[:, -CONV_HIST:])

    xs = xs_tm.reshape(t_new, nb, d).transpose(1, 0, 2)
    return (xp, xs, jnp.stack(kp_l), jnp.stack(vp_l), jnp.stack(cp_l),
            jnp.stack(ks_l), jnp.stack(vs_l), jnp.stack(cs_l))
```

```python
import functools
import math

import jax
import jax.numpy as jnp
from jax import lax
from jax.experimental import pallas as pl
from jax.experimental.pallas import tpu as pltpu

F32 = jnp.float32
BF16 = jnp.bfloat16

HEAD_DIM = 64
N_HEADS = 8
ATTN_W = N_HEADS * HEAD_DIM
CONV_K = 31
CONV_HIST = CONV_K - 1
PATTERNS = ((128, 1), (512, 4), (2048, 16))
WINDOW_MAX = 2048
BAND = 128
NUM_BUCKETS = 32
EPS = 1e-6
N_MOD = 6
LANES = 128
SUBLANES = 8
HALO = 32
CONV_PITCH = 3
VMEM_LIMIT = 56 * 1024 * 1024
N_SLABS = ATTN_W // LANES
SAMPLE_HEAD_GROUPS = 2

_NT = (((1,), (1,)), ((), ()))


def _dot(a, b):
    return jnp.dot(a, b, preferred_element_type=F32)


def _resident(shape):
    return pl.BlockSpec(shape, lambda *_: (0,) * len(shape), pipeline_mode=pl.Buffered(1))


def _params(sem, flags=None):
    return pltpu.CompilerParams(dimension_semantics=sem, vmem_limit_bytes=VMEM_LIMIT, flags=flags)


def _mod_kernel(c_ref, w_ref, b_ref, o_ref):
    c = c_ref[...]
    a = c * jax.nn.sigmoid(c)
    w = w_ref[...]
    a_hi = a.astype(BF16)
    a_lo = (a - a_hi.astype(F32)).astype(BF16)
    w_hi = w.astype(BF16)
    w_lo = (w - w_hi.astype(F32)).astype(BF16)
    o_ref[...] = _dot(a_hi, w_hi) + _dot(a_lo, w_hi) + _dot(a_hi, w_lo) + b_ref[...]


def _modulation(c_all, w_ada, b_ada):
    rows, d = c_all.shape
    n = w_ada.shape[1]
    tn = d
    return pl.pallas_call(
        _mod_kernel,
        grid=(n // tn,),
        in_specs=[pl.BlockSpec((rows, d), lambda j: (0, 0)),
                  pl.BlockSpec((d, tn), lambda j: (0, j)),
                  pl.BlockSpec((1, tn), lambda j: (0, j))],
        out_specs=pl.BlockSpec((rows, tn), lambda j: (0, j)),
        out_shape=jax.ShapeDtypeStruct((rows, n), F32),
        compiler_params=_params(("arbitrary",)),
        name="mod",
    )(c_all, w_ada, b_ada.reshape(1, n))


def _rel_bucket(dist):
    max_exact = NUM_BUCKETS // 2
    df = jnp.maximum(dist, 1).astype(F32)
    large = max_exact + (jnp.log(df / max_exact) / math.log(WINDOW_MAX / max_exact)
                         * (NUM_BUCKETS - max_exact)).astype(jnp.int32)
    large = jnp.minimum(large, NUM_BUCKETS - 1)
    return jnp.where(dist < max_exact, dist, large)


def _prompt_bias_kernel(rel_ref, bkt_ref, o_ref):
    h = pl.program_id(1)
    bkt = bkt_ref[0]
    acc = jnp.full(bkt.shape, -jnp.inf, F32)
    for b in range(NUM_BUCKETS):
        acc = jnp.where(bkt == b, rel_ref[b, h], acc)
    o_ref[0, 0] = acc


def _prompt_bias_tables(rel_bias):
    i = jnp.arange(BAND)[:, None]
    jj = jnp.arange(2 * BAND)[None, :]
    delta = BAND + i - jj
    valid = (delta >= 0) & (delta <= BAND)
    bkt = jnp.stack([jnp.where(valid, _rel_bucket(jnp.maximum(delta, 0) * d), -1)
                     for (_, d) in PATTERNS]).astype(jnp.int32)
    npat = len(PATTERNS)
    return pl.pallas_call(
        _prompt_bias_kernel,
        grid=(npat, N_HEADS),
        in_specs=[pl.BlockSpec(memory_space=pltpu.SMEM),
                  pl.BlockSpec((1, BAND, 2 * BAND), lambda p, h: (p, 0, 0))],
        out_specs=pl.BlockSpec((1, 1, BAND, 2 * BAND), lambda p, h: (p, h, 0, 0)),
        out_shape=jax.ShapeDtypeStruct((npat, N_HEADS, BAND, 2 * BAND), F32),
        compiler_params=_params(("arbitrary", "arbitrary")),
        name="prompt_bias",
    )(rel_bias, bkt)


def _sample_bias_kernel(relrow_ref, bkt_ref, o_ref):
    bkt = bkt_ref[0]
    reps = bkt.shape[1] // LANES
    acc = jnp.full(bkt.shape, -jnp.inf, F32)
    for b in range(NUM_BUCKETS):
        acc = jnp.where(bkt == b, jnp.tile(relrow_ref[b], (1, reps)), acc)
    o_ref[0] = acc


def _sample_bias_tables(rel_bias, n_hist, t_new):
    n_keys = n_hist + LANES
    c = jnp.arange(n_keys)[None, :]
    t = jnp.arange(t_new)[:, None]
    key = jnp.where(c < n_hist, c, c - (LANES - t_new))
    real = (c < n_hist) | (c >= n_keys - t_new)
    dist = n_hist + t - key
    tabs = []
    for (w, d) in PATTERNS:
        valid = real & (dist >= 0) & (dist <= w) & (dist % d == 0)
        tabs.append(jnp.where(valid, _rel_bucket(jnp.maximum(dist, 0)), -1))
    bkt = jnp.stack(tabs).astype(jnp.int32)
    bkt = jnp.repeat(bkt, N_HEADS, axis=1)
    relrow = jnp.broadcast_to(jnp.tile(rel_bias, (1, t_new))[:, :, None],
                              (NUM_BUCKETS, t_new * N_HEADS, LANES))
    npat = len(PATTERNS)
    rows = t_new * N_HEADS
    return pl.pallas_call(
        _sample_bias_kernel,
        grid=(npat,),
        in_specs=[pl.BlockSpec((NUM_BUCKETS, rows, LANES), lambda p: (0, 0, 0)),
                  pl.BlockSpec((1, rows, n_keys), lambda p: (p, 0, 0))],
        out_specs=pl.BlockSpec((1, rows, n_keys), lambda p: (p, 0, 0)),
        out_shape=jax.ShapeDtypeStruct((npat, rows, n_keys), F32),
        compiler_params=_params(("arbitrary",)),
        name="sample_bias",
    )(relrow, bkt)


def _modulated_rms(x, g, scale, shift):
    y = x * lax.rsqrt(jnp.mean(x * x, axis=-1, keepdims=True) + EPS) * g
    return y * (1.0 + scale) + shift


def _mod_slice(mod, k, d):
    return mod[:, k * d:(k + 1) * d]


def _head_norm(t, g, headmean_ref):
    ms = _dot((t * t).astype(BF16), headmean_ref[...])
    return t * lax.rsqrt(ms + EPS) * g


def _in_proj(hb, w_in_ref, qg_ref, kg_ref, headmean_ref):
    aw = ATTN_W
    q = _head_norm(_dot(hb, w_in_ref[:, 0:aw]), qg_ref[...], headmean_ref)
    k = _head_norm(_dot(hb, w_in_ref[:, aw:2 * aw]), kg_ref[...], headmean_ref)
    v = _dot(hb, w_in_ref[:, 2 * aw:3 * aw])
    cw = (w_in_ref.shape[1] - 3 * aw) // 2
    ua = _dot(hb, w_in_ref[:, 3 * aw:3 * aw + cw])
    ub = _dot(hb, w_in_ref[:, 3 * aw + cw:3 * aw + 2 * cw])
    return q, k, v, ua * jax.nn.sigmoid(ub)


def _conv_post(dw, lng_ref, lnb_ref):
    mu = jnp.mean(dw, axis=-1, keepdims=True)
    xc = dw - mu
    var = jnp.mean(xc * xc, axis=-1, keepdims=True)
    y = xc * lax.rsqrt(var + EPS) * lng_ref[...] + lnb_ref[...]
    return (y * jax.nn.sigmoid(y)).astype(BF16)


def _mlp_tail(x, mix, mod, n2g_ref, w1_ref, w2_ref, a_ref, ff_chunk):
    d = x.shape[-1]
    x1 = x + _mod_slice(mod, 2, d) * mix
    h2 = _modulated_rms(x1, n2g_ref[...], _mod_slice(mod, 4, d), _mod_slice(mod, 3, d)).astype(BF16)
    for c in range(w1_ref.shape[1] // ff_chunk):
        a = jnp.maximum(_dot(h2, w1_ref[:, c * ff_chunk:(c + 1) * ff_chunk]), 0.0)
        a_ref[:, c * ff_chunk:(c + 1) * ff_chunk] = (a * a).astype(BF16)
    return x1 + _mod_slice(mod, 5, d) * _dot(a_ref[...], w2_ref[...])


def _slabs(x):
    return [x[:, s * LANES:(s + 1) * LANES] for s in range(x.shape[1] // LANES)]


def _regroup_store(x, nat, p4, out1_ref, out4_ref, out16_ref):
    tm = x.shape[0]
    d4 = PATTERNS[1][1]
    d16 = PATTERNS[2][1]
    sub = d16 // d4
    out1_ref[0] = x.astype(BF16)
    for s, xs in enumerate(_slabs(x)):
        nat[s] = xs
    for r4 in range(d4):
        parts = []
        for s in range(N_SLABS):
            part = nat[s, pl.ds(r4, tm // d4, stride=d4), :]
            p4[s, r4] = part
            parts.append(part)
        out4_ref[0, r4] = jnp.concatenate(parts, axis=1).astype(BF16)
    for r16 in range(d16):
        r4, a = r16 % d4, r16 // d4
        parts = [p4[s, r4, pl.ds(a, tm // d16, stride=sub), :] for s in range(N_SLABS)]
        out16_ref[0, r16] = jnp.concatenate(parts, axis=1).astype(BF16)


def _front_kernel(x_ref, mod_ref, n1g_ref, w_in_ref, qg_ref, kg_ref, headmean_ref,
                  q1_ref, k1_ref, v1_ref, q4_ref, k4_ref, v4_ref, q16_ref, k16_ref, v16_ref,
                  glu_ref, kf_ref, vf_ref, nat, p4):
    d = x_ref.shape[2]
    mod = mod_ref[0]
    h = _modulated_rms(x_ref[0], n1g_ref[...], _mod_slice(mod, 1, d), _mod_slice(mod, 0, d))
    q, k, v, glu = _in_proj(h.astype(BF16), w_in_ref, qg_ref, kg_ref, headmean_ref)
    glu_ref[0] = glu
    kf_ref[0] = k
    vf_ref[0] = v
    _regroup_store(q * (HEAD_DIM ** -0.5), nat, p4, q1_ref, q4_ref, q16_ref)
    _regroup_store(k, nat, p4, k1_ref, k4_ref, k16_ref)
    _regroup_store(v, nat, p4, v1_ref, v4_ref, v16_ref)


def _prompt_front(x, mod, lw, tm=512):
    b, s, d = x.shape
    win = min(WINDOW_MAX, s)
    tail_start = (s - win) // tm
    cw = (lw["w_in"].shape[1] - 3 * ATTN_W) // 2
    d4, d16 = PATTERNS[1][1], PATTERNS[2][1]
    tok = lambda width: pl.BlockSpec((1, tm, width), lambda bi, i: (bi, i, 0))
    plane = lambda dil: pl.BlockSpec((1, dil, tm // dil, ATTN_W), lambda bi, i: (bi, 0, i, 0))
    tail = pl.BlockSpec((1, tm, ATTN_W), lambda bi, i: (bi, jnp.maximum(i - tail_start, 0), 0))
    in_specs = [tok(d),
                pl.BlockSpec((1, 1, N_MOD * d), lambda bi, i: (bi, 0, 0)),
                _resident((1, d)), _resident(lw["w_in"].shape), _resident((1, ATTN_W)),
                _resident((1, ATTN_W)), _resident((ATTN_W, ATTN_W))]
    out_specs = ([tok(ATTN_W)] * 3 + [plane(d4)] * 3 + [plane(d16)] * 3 + [tok(cw), tail, tail])
    nat_shape = jax.ShapeDtypeStruct((b, s, ATTN_W), BF16)
    plane_shape = lambda dil: jax.ShapeDtypeStruct((b, dil, s // dil, ATTN_W), BF16)
    out_shape = ([nat_shape] * 3 + [plane_shape(d4)] * 3 + [plane_shape(d16)] * 3 + [
        jax.ShapeDtypeStruct((b, s, cw), F32),
        jax.ShapeDtypeStruct((b, win, ATTN_W), F32), jax.ShapeDtypeStruct((b, win, ATTN_W), F32)])
    return pl.pallas_call(
        _front_kernel,
        grid=(b, s // tm),
        in_specs=in_specs, out_specs=out_specs, out_shape=out_shape,
        scratch_shapes=[pltpu.VMEM((N_SLABS, tm, LANES), F32),
                        pltpu.VMEM((N_SLABS, d4, tm // d4, LANES), F32)],
        compiler_params=_params(("arbitrary", "arbitrary")),
        name="prompt_front",
    )(x, mod.reshape(b, 1, N_MOD * d), lw["norm1_g"], lw["w_in"], lw["q_norm_g"], lw["k_norm_g"],
      lw["headmean"])


def _attn_kernel(q_ref, kp_ref, kc_ref, vp_ref, vc_ref, bias_ref, o_ref, st_ref, kbuf, vbuf):
    i = pl.program_id(2)
    qb = q_ref.shape[2]
    kbuf[0:BAND, :] = kp_ref[0, 0]
    kbuf[BAND:BAND + qb, :] = kc_ref[0, 0]
    vbuf[0:BAND, :] = vp_ref[0, 0]
    vbuf[BAND:BAND + qb, :] = vc_ref[0, 0]
    lane = lax.broadcasted_iota(jnp.int32, (BAND, LANES), 1)
    low = lane < HEAD_DIM
    col = lax.broadcasted_iota(jnp.int32, (1, 2 * BAND), 1)
    first_mask = jnp.where((col < BAND) & (i == 0), -jnp.inf, 0.0)
    zero = jnp.zeros((), BF16)
    for j in range(qb // BAND):
        rows = slice(j * BAND, (j + 1) * BAND)
        keys = slice(j * BAND, (j + 2) * BAND)
        stat = jnp.zeros((BAND, LANES), F32)
        for p in range(N_SLABS):
            cols = slice(p * LANES, (p + 1) * LANES)
            q2 = q_ref[0, 0, rows, cols]
            k2 = kbuf[keys, cols]
            v2 = vbuf[keys, cols]
            halves = []
            for e in range(LANES // HEAD_DIM):
                hd = p * (LANES // HEAD_DIM) + e
                qm = jnp.where(low if e == 0 else jnp.logical_not(low), q2, zero)
                s = lax.dot_general(qm, k2, _NT, preferred_element_type=F32) + bias_ref[hd]
                if j == 0:
                    s = s + first_mask
                m = jnp.max(s, axis=-1, keepdims=True)
                ex = jnp.exp(s - m)
                l = jnp.sum(ex, axis=-1, keepdims=True)
                halves.append(_dot(ex.astype(BF16), v2) * (1.0 / l))
                stat = jnp.where(lane == hd, m + jnp.log(l), stat)
            o_ref[0, 0, rows, cols] = jnp.where(low, halves[0], halves[1]).astype(BF16)
        st_ref[0, 0, rows, :] = stat


def _prompt_attention(q, k, v, bias, qb=512):
    b, dil, u, _ = q.shape
    qb = min(qb, u)
    ratio = qb // BAND
    cur = pl.BlockSpec((1, 1, qb, ATTN_W), lambda bi, r, i: (bi, r, i, 0))
    prev = pl.BlockSpec((1, 1, BAND, ATTN_W),
                        lambda bi, r, i: (bi, r, jnp.maximum(i * ratio - 1, 0), 0))
    return pl.pallas_call(
        _attn_kernel,
        grid=(b, dil, u // qb),
        in_specs=[cur, prev, cur, prev, cur, _resident(bias.shape)],
        out_specs=[cur, pl.BlockSpec((1, 1, qb, LANES), lambda bi, r, i: (bi, r, i, 0))],
        out_shape=[jax.ShapeDtypeStruct((b, dil, u, ATTN_W), BF16),
                   jax.ShapeDtypeStruct((b, dil, u, LANES), F32)],
        scratch_shapes=[pltpu.VMEM((BAND + qb, ATTN_W), BF16), pltpu.VMEM((BAND + qb, ATTN_W), BF16)],
        compiler_params=_params(("arbitrary", "arbitrary", "arbitrary")),
        name=f"prompt_attn_d{dil}",
    )(q, k, k, v, v, bias)


def _back_kernel(x_ref, mod_ref, o1_ref, o4_ref, o16_ref, s1_ref, s4_ref, s16_ref,
                 glu_ref, halo_ref, cw_ref, cb_ref, lng_ref, lnb_ref, wpw_ref,
                 expand_ref, wo_ref, n2g_ref, w1_ref, w2_ref,
                 qbd_ref, kn_ref, vn_ref, ck_ref, cv_ref, tab_ref, hmask_ref,
                 y_ref, nk_ref, nv_ref, os_ref,
                 lnat, onat, gbuf, act, a_ref, kb, vb, *, row_chunk, ff_chunk, t_new, n_groups):
    tm = x_ref.shape[1]
    group = (pl.program_id(0) * pl.num_programs(1) + pl.program_id(1)) % n_groups
    _window_attend(qbd_ref[0, 0], lambda p: tab_ref[group, p],
                   kn_ref.at[0], vn_ref.at[0], ck_ref.at[0], cv_ref.at[0], hmask_ref,
                   nk_ref.at[0], nv_ref.at[0], os_ref.at[0, 0], kb, vb,
                   t_new=t_new, n_patterns=tab_ref.shape[1], row_chunk=row_chunk)

    def steps(first, count):
        return pl.ds(CONV_PITCH * first, count, stride=CONV_PITCH)

    n_cs = gbuf.shape[0]
    has_prev = (pl.program_id(1) > 0).astype(F32)
    for s, part in enumerate(_slabs(halo_ref[0] * has_prev)):
        gbuf[s, steps(0, HALO), :] = part
    for s, part in enumerate(_slabs(glu_ref[0])):
        gbuf[s, steps(HALO, tm), :] = part
    for which, (o_ref, s_ref) in enumerate(((o4_ref, s4_ref), (o16_ref, s16_ref))):
        dil = o_ref.shape[1]
        for r in range(dil):
            rows = pl.ds(r, tm // dil, stride=dil)
            lnat[which, rows, :] = s_ref[0, r]
            for s, part in enumerate(_slabs(o_ref[0, r])):
                onat[which, s, rows, :] = part.astype(F32)

    lead = HALO - CONV_HIST
    for c in range(tm // row_chunk):
        accs = [jnp.broadcast_to(b_s, (row_chunk, LANES)) for b_s in _slabs(cb_ref[...])]
        for j in range(CONV_K):
            for s in range(n_cs):
                accs[s] = accs[s] + (gbuf[s, steps(c * row_chunk + lead + j, row_chunk), :]
                                     * cw_ref[j:j + 1, s * LANES:(s + 1) * LANES])
        act[c * row_chunk:(c + 1) * row_chunk, :] = _conv_post(
            jnp.concatenate(accs, axis=1), lng_ref, lnb_ref)
    o_conv = _dot(act[...], wpw_ref[...]).astype(BF16)

    ls = [s1_ref[0, 0], lnat[0], lnat[1]]
    top = jnp.maximum(jnp.maximum(ls[0], ls[1]), ls[2])
    ps = [jnp.exp(t - top) for t in ls]
    inv = 1.0 / (ps[0] + ps[1] + ps[2])
    os = [o1_ref[0, 0].astype(F32)] + [
        jnp.concatenate([onat[which, s] for s in range(N_SLABS)], axis=1) for which in range(2)]
    o_attn = None
    for p_d, o_d in zip(ps, os):
        w = p_d * inv
        w_hi = w.astype(BF16)
        w_lo = (w - w_hi.astype(F32)).astype(BF16)
        w_full = _dot(w_hi, expand_ref[...]) + _dot(w_lo, expand_ref[...])
        term = w_full * o_d
        o_attn = term if o_attn is None else o_attn + term
    aw = o_attn.shape[1]
    mix = _dot(o_attn.astype(BF16), wo_ref[0:aw, :]) + _dot(o_conv, wo_ref[aw:, :])
    y_ref[0] = _mlp_tail(x_ref[0], mix, mod_ref[0], n2g_ref, w1_ref, w2_ref, a_ref, ff_chunk)


def _prompt_back(x, mod, outs, stats, glu, lw, sample, row_chunk=64, ff_chunk=512):
    b, s, d = x.shape
    dff = lw["w_ff1"].shape[1]
    cw = glu.shape[2]
    qbd, k_new_t, v_new_t, cache_k_t, cache_v_t, tabs, hmask = sample
    nb, n_groups, rows_g, gw = qbd.shape
    n_hist = cache_k_t.shape[2]
    t_new = rows_g // (gw // HEAD_DIM)
    tm = (b * s) // (nb * n_groups)
    assert b * s == tm * nb * n_groups and s % tm == 0 and tm % row_chunk == 0
    n = s // tm
    unit = lambda bi, i: ((bi * n + i) // n_groups, (bi * n + i) % n_groups)
    hosted = lambda shape: pl.BlockSpec((1,) + shape, lambda bi, i: unit(bi, i) + (0,))
    hosted4 = lambda shape: pl.BlockSpec((1, 1) + shape, lambda bi, i: unit(bi, i) + (0, 0))
    tok = lambda width: pl.BlockSpec((1, tm, width), lambda bi, i: (bi, i, 0))
    plane = lambda a: pl.BlockSpec((1, a.shape[1], tm // a.shape[1], a.shape[3]),
                                   lambda bi, i: (bi, 0, i, 0))
    halo = pl.BlockSpec((1, HALO, cw), lambda bi, i: (bi, jnp.maximum(i * (tm // HALO) - 1, 0), 0))
    in_specs = ([tok(d), pl.BlockSpec((1, 1, N_MOD * d), lambda bi, i: (bi, 0, 0))]
                + [plane(a) for a in outs] + [plane(a) for a in stats]
                + [tok(cw), halo,
                   _resident((CONV_K, cw)), _resident((1, cw)), _resident((1, cw)), _resident((1, cw)),
                   _resident((cw, cw)),
                   _resident((LANES, ATTN_W)), _resident((d, d)), _resident((1, d)),
                   _resident((d, dff)), _resident((dff, d)),
                   hosted4((rows_g, gw)), hosted((gw, LANES)), hosted((gw, LANES)),
                   hosted((gw, n_hist)), hosted((gw, n_hist)),
                   _resident(tabs.shape), _resident(hmask.shape)])
    out_specs = [tok(d), hosted((gw, n_hist)), hosted((gw, n_hist)), hosted4((t_new, gw))]
    out_shape = [jax.ShapeDtypeStruct((b, s, d), F32),
                 jax.ShapeDtypeStruct(cache_k_t.shape, cache_k_t.dtype),
                 jax.ShapeDtypeStruct(cache_v_t.shape, cache_v_t.dtype),
                 jax.ShapeDtypeStruct((nb, n_groups, t_new, gw), F32)]
    return pl.pallas_call(
        functools.partial(_back_kernel, row_chunk=row_chunk, ff_chunk=ff_chunk,
                          t_new=t_new, n_groups=n_groups),
        grid=(b, n),
        in_specs=in_specs, out_specs=out_specs, out_shape=out_shape,
        scratch_shapes=[pltpu.VMEM((2, tm, LANES), F32),
                        pltpu.VMEM((2, N_SLABS, tm, LANES), F32),
                        pltpu.VMEM((cw // LANES, CONV_PITCH * (tm + HALO), LANES), F32),
                        pltpu.VMEM((tm, cw), BF16), pltpu.VMEM((tm, dff), BF16),
                        pltpu.VMEM((gw, n_hist + LANES), BF16), pltpu.VMEM((gw, n_hist + LANES), BF16)],
        compiler_params=_params(("arbitrary", "arbitrary")),
        name="prompt_back",
    )(x, mod.reshape(b, 1, N_MOD * d), *outs, *stats, glu, glu, lw["conv_dw"], lw["conv_dw_b"], lw["conv_ln_g"],
      lw["conv_ln_b"], lw["w_pw2"], lw["expand"], lw["w_o"], lw["norm2_g"], lw["w_ff1"], lw["w_ff2"],
      qbd, k_new_t, v_new_t, cache_k_t, cache_v_t, tabs, hmask)


def _sample_front_kernel(x_ref, mod_ref, hist_ref, n1g_ref, w_in_ref, qg_ref, kg_ref, headmean_ref,
                         cw_ref, cb_ref, lng_ref, lnb_ref, wpw_ref,
                         q_ref, k_ref, v_ref, glu_ref, oc_ref, hbuf, act, *, t_new):
    nb = mod_ref.shape[0]
    d = x_ref.shape[1]
    mod = mod_ref[...]
    for t in range(t_new):
        rows = slice(t * nb, (t + 1) * nb)
        h = _modulated_rms(x_ref[rows, :], n1g_ref[...], _mod_slice(mod, 1, d), _mod_slice(mod, 0, d))
        hbuf[rows, :] = h.astype(BF16)
    q, k, v, glu = _in_proj(hbuf[...], w_in_ref, qg_ref, kg_ref, headmean_ref)
    q_ref[...] = q
    k_ref[...] = k
    v_ref[...] = v
    glu_ref[...] = glu

    def slab(sidx):
        if sidx < CONV_HIST:
            return hist_ref[sidx]
        return glu_ref[(sidx - CONV_HIST) * nb:(sidx - CONV_HIST + 1) * nb, :]

    for t in range(t_new):
        acc = jnp.broadcast_to(cb_ref[...], (nb, glu.shape[1]))
        for j in range(CONV_K):
            acc = acc + slab(t + j) * cw_ref[j:j + 1, :]
        act[t * nb:(t + 1) * nb, :] = _conv_post(acc, lng_ref, lnb_ref)
    oc_ref[...] = _dot(act[...], wpw_ref[...]).astype(BF16)


def _sample_front(x_tm, mod, hist_tm, lw, t_new):
    rows, d = x_tm.shape
    cw = lw["conv_dw"].shape[1]
    full = lambda shape: pl.BlockSpec(shape, lambda i: (0,) * len(shape))
    args = (x_tm, mod, hist_tm, lw["norm1_g"], lw["w_in"], lw["q_norm_g"], lw["k_norm_g"],
            lw["headmean"], lw["conv_dw"], lw["conv_dw_b"], lw["conv_ln_g"], lw["conv_ln_b"],
            lw["w_pw2"])
    return pl.pallas_call(
        functools.partial(_sample_front_kernel, t_new=t_new),
        grid=(1,),
        in_specs=[full(a.shape) for a in args],
        out_specs=[full((rows, ATTN_W))] * 3 + [full((rows, cw)), full((rows, cw))],
        out_shape=[jax.ShapeDtypeStruct((rows, ATTN_W), F32)] * 3 + [
            jax.ShapeDtypeStruct((rows, cw), F32), jax.ShapeDtypeStruct((rows, cw), BF16)],
        scratch_shapes=[pltpu.VMEM((rows, d), BF16), pltpu.VMEM((rows, cw), BF16)],
        compiler_params=_params(("arbitrary",)),
        name="sample_front",
    )(*args)


def _window_attend(qbd, table, kn_ref, vn_ref, ck_ref, cv_ref, hmask_ref, nk_ref, nv_ref, o_ref,
                   kb, vb, *, t_new, n_patterns, row_chunk):
    width, n_hist = ck_ref.shape
    keep = n_hist - LANES
    lane = lax.broadcasted_iota(jnp.int32, (row_chunk, LANES), 1)
    is_new = lane >= LANES - t_new
    for src_ref, new_ref, dst_ref, buf in ((ck_ref, kn_ref, nk_ref, kb), (cv_ref, vn_ref, nv_ref, vb)):
        for c in range(0, width, row_chunk):
            rows = slice(c, c + row_chunk)
            old = src_ref[rows, :]
            new = new_ref[rows, :]
            buf[rows, 0:n_hist] = old.astype(BF16)
            buf[rows, n_hist:] = new.astype(BF16)
            shifted = pltpu.roll(old, n_hist - t_new, axis=1)
            dst_ref[rows, 0:keep] = shifted[:, 0:keep]
            dst_ref[rows, keep:] = jnp.where(is_new, new, shifted[:, keep:])

    s = _dot(qbd, kb[...])
    ms, ls, exs = [], [], []
    for p in range(n_patterns):
        sp = s + table(p)
        m = jnp.max(sp, axis=-1, keepdims=True)
        ex = jnp.exp(sp - m)
        ms.append(m)
        ls.append(jnp.sum(ex, axis=-1, keepdims=True))
        exs.append(ex.astype(BF16))
    pv = lax.dot_general(jnp.concatenate(exs, axis=0), vb[...], _NT, preferred_element_type=F32)
    nr = s.shape[0]
    accs = [pv[p * nr:(p + 1) * nr, :] for p in range(n_patterns)]
    top = functools.reduce(jnp.maximum, ms)
    ws = [jnp.exp(m - top) for m in ms]
    num = sum(w * a for w, a in zip(ws, accs))
    den = sum(w * l for w, l in zip(ws, ls))
    rows = num * (1.0 / den) * hmask_ref[...]
    heads = nr // t_new
    for t in range(t_new):
        o_ref[t:t + 1, :] = jnp.sum(rows[t * heads:(t + 1) * heads, :], axis=0, keepdims=True)


def _sample_back_kernel(x_ref, mod_ref, oa_ref, oc_ref, wo_ref, n2g_ref, w1_ref, w2_ref,
                        y_ref, a_ref, *, ff_chunk):
    aw = oa_ref.shape[1]
    mix = _dot(oa_ref[...].astype(BF16), wo_ref[0:aw, :]) + _dot(oc_ref[...], wo_ref[aw:, :])
    y_ref[...] = _mlp_tail(x_ref[...], mix, mod_ref[...], n2g_ref, w1_ref, w2_ref, a_ref, ff_chunk)


def _sample_back(x_tm, mod, o_attn_tm, o_conv_tm, lw, t_new, ff_chunk=512):
    rows, d = x_tm.shape
    nb = rows // t_new
    dff = lw["w_ff1"].shape[1]
    cw = o_conv_tm.shape[1]
    tok = lambda width: pl.BlockSpec((nb, width), lambda t: (t, 0))
    return pl.pallas_call(
        functools.partial(_sample_back_kernel, ff_chunk=ff_chunk),
        grid=(t_new,),
        in_specs=[tok(d), _resident(mod.shape), tok(ATTN_W), tok(cw),
                  _resident((d, d)), _resident((1, d)), _resident((d, dff)), _resident((dff, d))],
        out_specs=tok(d),
        out_shape=jax.ShapeDtypeStruct((rows, d), F32),
        scratch_shapes=[pltpu.VMEM((nb, dff), BF16)],
        compiler_params=_params(("arbitrary",)),
        name="sample_back",
    )(x_tm, mod, o_attn_tm, o_conv_tm, lw["w_o"], lw["norm2_g"], lw["w_ff1"], lw["w_ff2"])


def _layer_weights(l, norm1_g, norm2_g, w_in, q_norm_g, k_norm_g, conv_dw, conv_dw_b,
                   conv_ln_g, conv_ln_b, w_pw2, w_o, w_ff1, w_ff2):
    head_of = jnp.arange(ATTN_W) // HEAD_DIM
    headmean = (head_of[:, None] == head_of[None, :]).astype(F32) / HEAD_DIM
    expand = (jnp.arange(LANES)[:, None] == head_of[None, :]).astype(BF16)
    row = lambda a: a[l].reshape(1, -1)
    return {
        "norm1_g": row(norm1_g), "norm2_g": row(norm2_g),
        "w_in": w_in[l].astype(BF16),
        "q_norm_g": jnp.tile(q_norm_g[l], N_HEADS).reshape(1, ATTN_W),
        "k_norm_g": jnp.tile(k_norm_g[l], N_HEADS).reshape(1, ATTN_W),
        "headmean": headmean.astype(BF16), "expand": expand,
        "conv_dw": conv_dw[l], "conv_dw_b": row(conv_dw_b),
        "conv_ln_g": row(conv_ln_g), "conv_ln_b": row(conv_ln_b),
        "w_pw2": w_pw2[l].astype(BF16), "w_o": w_o[l].astype(BF16),
        "w_ff1": w_ff1[l].astype(BF16), "w_ff2": w_ff2[l].astype(BF16),
    }


def _feature_major(cache):
    nb, n_hist = cache.shape[0], cache.shape[1]
    return cache.transpose(0, 2, 3, 1).reshape(nb, ATTN_W, n_hist)


def _position_major(cache_t):
    nb, _, n_hist = cache_t.shape
    return cache_t.reshape(nb, N_HEADS, HEAD_DIM, n_hist).transpose(0, 3, 1, 2)


def kernel(x_prompt, x_sample, cache_k, cache_v, state_conv, c_prompt, c_sample, rel_bias, norm1_g, norm2_g, w_ada, b_ada, w_in, q_norm_g, k_norm_g, conv_dw, conv_dw_b, conv_ln_g, conv_ln_b, w_pw2, w_o, w_ff1, w_ff2):
    depth = w_in.shape[0]
    b, s, d = x_prompt.shape
    nb, t_new, _ = x_sample.shape
    n_hist = cache_k.shape[2]
    assert s % WINDOW_MAX == 0 and n_hist % LANES == 0 and t_new <= LANES

    prompt_bias = _prompt_bias_tables(rel_bias)
    n_groups = SAMPLE_HEAD_GROUPS
    hpg = N_HEADS // n_groups
    gw = hpg * HEAD_DIM
    n_keys = n_hist + LANES
    sample_tabs = _sample_bias_tables(rel_bias, n_hist, t_new)
    sample_tabs = sample_tabs.reshape(len(PATTERNS), t_new, n_groups, hpg, n_keys).transpose(
        2, 0, 1, 3, 4).reshape(n_groups, len(PATTERNS), t_new * hpg, n_keys)
    hmask_g = (jnp.arange(hpg)[:, None] == (jnp.arange(gw) // HEAD_DIM)[None, :]).astype(F32)
    hmask = jnp.tile(hmask_g, (t_new, 1))

    xp = x_prompt
    xs_tm = x_sample.transpose(1, 0, 2).reshape(t_new * nb, d)
    c_all = jnp.concatenate([c_prompt, c_sample], axis=0)
    kp_l, vp_l, cp_l, ks_l, vs_l, cs_l = [], [], [], [], [], []
    for l in range(depth):
        lw = _layer_weights(l, norm1_g, norm2_g, w_in, q_norm_g, k_norm_g, conv_dw, conv_dw_b,
                            conv_ln_g, conv_ln_b, w_pw2, w_o, w_ff1, w_ff2)
        mod = _modulation(c_all, w_ada[l], b_ada[l])
        mod_p, mod_s = mod[:b], mod[b:]

        hist_tm = state_conv[l].transpose(1, 0, 2)
        q_s, k_s, v_s, glu_s, oc_s = _sample_front(xs_tm, mod_s, hist_tm, lw, t_new)
        to_bm = lambda a: a.reshape(t_new, nb, -1).transpose(1, 0, 2)
        q_bm = (to_bm(q_s) * (HEAD_DIM ** -0.5)).reshape(nb, t_new, n_groups, gw).transpose(0, 2, 1, 3)
        qbd = (q_bm[:, :, :, None, :] * hmask_g[None, None, None]).reshape(nb, n_groups, t_new * hpg, gw)
        new_t = lambda a: jnp.pad(a.reshape(t_new, nb, ATTN_W).transpose(1, 2, 0),
                                  ((0, 0), (0, 0), (LANES - t_new, 0)))
        sample = (qbd.astype(BF16), new_t(k_s), new_t(v_s),
                  _feature_major(cache_k[l]), _feature_major(cache_v[l]), sample_tabs, hmask)

        (q1, k1, v1, q4, k4, v4, q16, k16, v16, glu, k_tail, v_tail) = _prompt_front(xp, mod_p, lw)
        unit = lambda a: a.reshape(b, 1, s, ATTN_W)
        outs, stats = [], []
        for pi, qkv in enumerate(((unit(q1), unit(k1), unit(v1)), (q4, k4, v4), (q16, k16, v16))):
            o_d, st_d = _prompt_attention(*qkv, prompt_bias[pi])
            outs.append(o_d)
            stats.append(st_d)
        xp, new_k_t, new_v_t, o_s = _prompt_back(xp, mod_p, outs, stats, glu, lw, sample)
        kp_l.append(k_tail.reshape(b, -1, N_HEADS, HEAD_DIM))
        vp_l.append(v_tail.reshape(b, -1, N_HEADS, HEAD_DIM))
        cp_l.append(glu[:, s - CONV_HIST:, :])

        o_s_tm = o_s.transpose(2, 0, 1, 3).reshape(t_new * nb, ATTN_W)
        xs_tm = _sample_back(xs_tm, mod_s, o_s_tm, oc_s, lw, t_new)
        ks_l.append(_position_major(new_k_t))
        vs_l.append(_position_major(new_v_t))
        cs_l.append(jnp.concatenate([state_conv[l].astype(F32), to_bm(glu_s)], axis=1)[:, -CONV_HIST:])

    xs = xs_tm.reshape(t_new, nb, d).transpose(1, 0, 2)
    return (xp, xs, jnp.stack(kp_l), jnp.stack(vp_l), jnp.stack(cp_l),
            jnp.stack(ks_l), jnp.stack(vs_l), jnp.stack(cs_l))
```

```python
import functools
import math

import jax
import jax.numpy as jnp
from jax import lax
from jax.experimental import pallas as pl
from jax.experimental.pallas import tpu as pltpu

F32 = jnp.float32
BF16 = jnp.bfloat16

HEAD_DIM = 64
N_HEADS = 8
ATTN_W = N_HEADS * HEAD_DIM
CONV_K = 31
CONV_HIST = CONV_K - 1
PATTERNS = ((128, 1), (512, 4), (2048, 16))
WINDOW_MAX = 2048
BAND = 128
NUM_BUCKETS = 32
EPS = 1e-6
N_MOD = 6
LANES = 128
SUBLANES = 8
HALO = 32
CONV_PITCH = 3
VMEM_LIMIT = 56 * 1024 * 1024
N_SLABS = ATTN_W // LANES
SAMPLE_HEAD_GROUPS = 2

_NT = (((1,), (1,)), ((), ()))


def _dot(a, b):
    return jnp.dot(a, b, preferred_element_type=F32)


def _resident(shape):
    return pl.BlockSpec(shape, lambda *_: (0,) * len(shape), pipeline_mode=pl.Buffered(1))


def _params(sem, flags=None):
    return pltpu.CompilerParams(dimension_semantics=sem, vmem_limit_bytes=VMEM_LIMIT, flags=flags)


def _mod_kernel(c_ref, w_ref, b_ref, o_ref):
    c = c_ref[...]
    a = c * jax.nn.sigmoid(c)
    w = w_ref[...]
    a_hi = a.astype(BF16)
    a_lo = (a - a_hi.astype(F32)).astype(BF16)
    w_hi = w.astype(BF16)
    w_lo = (w - w_hi.astype(F32)).astype(BF16)
    o_ref[...] = _dot(a_hi, w_hi) + _dot(a_lo, w_hi) + _dot(a_hi, w_lo) + b_ref[...]


def _modulation(c_all, w_ada, b_ada):
    rows, d = c_all.shape
    n = w_ada.shape[1]
    tn = d
    return pl.pallas_call(
        _mod_kernel,
        grid=(n // tn,),
        in_specs=[pl.BlockSpec((rows, d), lambda j: (0, 0)),
                  pl.BlockSpec((d, tn), lambda j: (0, j)),
                  pl.BlockSpec((1, tn), lambda j: (0, j))],
        out_specs=pl.BlockSpec((rows, tn), lambda j: (0, j)),
        out_shape=jax.ShapeDtypeStruct((rows, n), F32),
        compiler_params=_params(("arbitrary",)),
        name="mod",
    )(c_all, w_ada, b_ada.reshape(1, n))


def _rel_bucket(dist):
    max_exact = NUM_BUCKETS // 2
    df = jnp.maximum(dist, 1).astype(F32)
    large = max_exact + (jnp.log(df / max_exact) / math.log(WINDOW_MAX / max_exact)
                         * (NUM_BUCKETS - max_exact)).astype(jnp.int32)
    large = jnp.minimum(large, NUM_BUCKETS - 1)
    return jnp.where(dist < max_exact, dist, large)


def _prompt_bias_kernel(rel_ref, bkt_ref, o_ref):
    bkt = bkt_ref[0]
    for h in range(o_ref.shape[1]):
        acc = jnp.full(bkt.shape, -jnp.inf, F32)
        for b in range(NUM_BUCKETS):
            acc = jnp.where(bkt == b, rel_ref[b, h], acc)
        o_ref[0, h] = acc


def _prompt_bias_tables(rel_bias):
    i = jnp.arange(BAND)[:, None]
    jj = jnp.arange(2 * BAND)[None, :]
    delta = BAND + i - jj
    valid = (delta >= 0) & (delta <= BAND)
    bkt = jnp.stack([jnp.where(valid, _rel_bucket(jnp.maximum(delta, 0) * d), -1)
                     for (_, d) in PATTERNS]).astype(jnp.int32)
    npat = len(PATTERNS)
    return pl.pallas_call(
        _prompt_bias_kernel,
        grid=(npat,),
        in_specs=[pl.BlockSpec(memory_space=pltpu.SMEM),
                  pl.BlockSpec((1, BAND, 2 * BAND), lambda p: (p, 0, 0))],
        out_specs=pl.BlockSpec((1, N_HEADS, BAND, 2 * BAND), lambda p: (p, 0, 0, 0)),
        out_shape=jax.ShapeDtypeStruct((npat, N_HEADS, BAND, 2 * BAND), F32),
        compiler_params=_params(("arbitrary",)),
        name="prompt_bias",
    )(rel_bias, bkt)


def _sample_bias_kernel(relrow_ref, bkt_ref, o_ref):
    bkt = bkt_ref[0]
    reps = bkt.shape[1] // LANES
    acc = jnp.full(bkt.shape, -jnp.inf, F32)
    for b in range(NUM_BUCKETS):
        acc = jnp.where(bkt == b, jnp.tile(relrow_ref[b], (1, reps)), acc)
    o_ref[0] = acc


def _sample_bias_tables(rel_bias, n_hist, t_new):
    n_keys = n_hist + LANES
    c = jnp.arange(n_keys)[None, :]
    t = jnp.arange(t_new)[:, None]
    key = jnp.where(c < n_hist, c, c - (LANES - t_new))
    real = (c < n_hist) | (c >= n_keys - t_new)
    dist = n_hist + t - key
    tabs = []
    for (w, d) in PATTERNS:
        valid = real & (dist >= 0) & (dist <= w) & (dist % d == 0)
        tabs.append(jnp.where(valid, _rel_bucket(jnp.maximum(dist, 0)), -1))
    bkt = jnp.stack(tabs).astype(jnp.int32)
    bkt = jnp.repeat(bkt, N_HEADS, axis=1)
    relrow = jnp.broadcast_to(jnp.tile(rel_bias, (1, t_new))[:, :, None],
                              (NUM_BUCKETS, t_new * N_HEADS, LANES))
    npat = len(PATTERNS)
    rows = t_new * N_HEADS
    return pl.pallas_call(
        _sample_bias_kernel,
        grid=(npat,),
        in_specs=[pl.BlockSpec((NUM_BUCKETS, rows, LANES), lambda p: (0, 0, 0)),
                  pl.BlockSpec((1, rows, n_keys), lambda p: (p, 0, 0))],
        out_specs=pl.BlockSpec((1, rows, n_keys), lambda p: (p, 0, 0)),
        out_shape=jax.ShapeDtypeStruct((npat, rows, n_keys), F32),
        compiler_params=_params(("arbitrary",)),
        name="sample_bias",
    )(relrow, bkt)


def _modulated_rms(x, g, scale, shift):
    y = x * lax.rsqrt(jnp.mean(x * x, axis=-1, keepdims=True) + EPS) * g
    return y * (1.0 + scale) + shift


def _mod_slice(mod, k, d):
    return mod[:, k * d:(k + 1) * d]


def _head_norm(t, g, headmean_ref):
    ms = _dot((t * t).astype(BF16), headmean_ref[...])
    return t * lax.rsqrt(ms + EPS) * g


def _in_proj(hb, w_in_ref, qg_ref, kg_ref, headmean_ref):
    aw = ATTN_W
    q = _head_norm(_dot(hb, w_in_ref[:, 0:aw]), qg_ref[...], headmean_ref)
    k = _head_norm(_dot(hb, w_in_ref[:, aw:2 * aw]), kg_ref[...], headmean_ref)
    v = _dot(hb, w_in_ref[:, 2 * aw:3 * aw])
    cw = (w_in_ref.shape[1] - 3 * aw) // 2
    ua = _dot(hb, w_in_ref[:, 3 * aw:3 * aw + cw])
    ub = _dot(hb, w_in_ref[:, 3 * aw + cw:3 * aw + 2 * cw])
    return q, k, v, ua * jax.nn.sigmoid(ub)


def _conv_post(dw, lng_ref, lnb_ref):
    mu = jnp.mean(dw, axis=-1, keepdims=True)
    xc = dw - mu
    var = jnp.mean(xc * xc, axis=-1, keepdims=True)
    y = xc * lax.rsqrt(var + EPS) * lng_ref[...] + lnb_ref[...]
    return (y * jax.nn.sigmoid(y)).astype(BF16)


def _mlp_tail(x, mix, mod, n2g_ref, w1_ref, w2_ref, a_ref, ff_chunk):
    d = x.shape[-1]
    x1 = x + _mod_slice(mod, 2, d) * mix
    h2 = _modulated_rms(x1, n2g_ref[...], _mod_slice(mod, 4, d), _mod_slice(mod, 3, d)).astype(BF16)
    for c in range(w1_ref.shape[1] // ff_chunk):
        a = jnp.maximum(_dot(h2, w1_ref[:, c * ff_chunk:(c + 1) * ff_chunk]), 0.0)
        a_ref[:, c * ff_chunk:(c + 1) * ff_chunk] = (a * a).astype(BF16)
    return x1 + _mod_slice(mod, 5, d) * _dot(a_ref[...], w2_ref[...])


def _slabs(x):
    return [x[:, s * LANES:(s + 1) * LANES] for s in range(x.shape[1] // LANES)]


def _regroup_store(x, nat, p4, out1_ref, out4_ref, out16_ref, which):
    tm = x.shape[0]
    d4 = PATTERNS[1][1]
    d16 = PATTERNS[2][1]
    sub = d16 // d4
    cols = slice(which * ATTN_W, (which + 1) * ATTN_W)
    out1_ref[0, :, cols] = x.astype(BF16)
    for s, xs in enumerate(_slabs(x)):
        nat[s] = xs
    for r4 in range(d4):
        parts = []
        for s in range(N_SLABS):
            part = nat[s, pl.ds(r4, tm // d4, stride=d4), :]
            p4[s, r4] = part
            parts.append(part)
        out4_ref[0, r4, :, cols] = jnp.concatenate(parts, axis=1).astype(BF16)
    for r16 in range(d16):
        r4, a = r16 % d4, r16 // d4
        parts = [p4[s, r4, pl.ds(a, tm // d16, stride=sub), :] for s in range(N_SLABS)]
        out16_ref[0, r16, :, cols] = jnp.concatenate(parts, axis=1).astype(BF16)


def _front_kernel(x_ref, mod_ref, n1g_ref, w_in_ref, qg_ref, kg_ref, headmean_ref,
                  qkv1_ref, qkv4_ref, qkv16_ref, glu_ref, kf_ref, vf_ref, nat, p4):
    d = x_ref.shape[2]
    mod = mod_ref[0]
    h = _modulated_rms(x_ref[0], n1g_ref[...], _mod_slice(mod, 1, d), _mod_slice(mod, 0, d))
    q, k, v, glu = _in_proj(h.astype(BF16), w_in_ref, qg_ref, kg_ref, headmean_ref)
    glu_ref[0] = glu
    kf_ref[0] = k
    vf_ref[0] = v
    for which, t in enumerate((q * (HEAD_DIM ** -0.5), k, v)):
        _regroup_store(t, nat, p4, qkv1_ref, qkv4_ref, qkv16_ref, which)


def _prompt_front(x, mod, lw, tm=512):
    b, s, d = x.shape
    win = min(WINDOW_MAX, s)
    tail_start = (s - win) // tm
    cw = (lw["w_in"].shape[1] - 3 * ATTN_W) // 2
    d4, d16 = PATTERNS[1][1], PATTERNS[2][1]
    tok = lambda width: pl.BlockSpec((1, tm, width), lambda bi, i: (bi, i, 0))
    qkv_w = 3 * ATTN_W
    plane = lambda dil: pl.BlockSpec((1, dil, tm // dil, qkv_w), lambda bi, i: (bi, 0, i, 0))
    tail = pl.BlockSpec((1, tm, ATTN_W), lambda bi, i: (bi, jnp.maximum(i - tail_start, 0), 0))
    in_specs = [tok(d),
                pl.BlockSpec((1, 1, N_MOD * d), lambda bi, i: (bi, 0, 0)),
                _resident((1, d)), _resident(lw["w_in"].shape), _resident((1, ATTN_W)),
                _resident((1, ATTN_W)), _resident((ATTN_W, ATTN_W))]
    out_specs = [tok(qkv_w), plane(d4), plane(d16), tok(cw), tail, tail]
    plane_shape = lambda dil: jax.ShapeDtypeStruct((b, dil, s // dil, qkv_w), BF16)
    out_shape = ([jax.ShapeDtypeStruct((b, s, qkv_w), BF16), plane_shape(d4), plane_shape(d16)] + [
        jax.ShapeDtypeStruct((b, s, cw), F32),
        jax.ShapeDtypeStruct((b, win, ATTN_W), F32), jax.ShapeDtypeStruct((b, win, ATTN_W), F32)])
    return pl.pallas_call(
        _front_kernel,
        grid=(b, s // tm),
        in_specs=in_specs, out_specs=out_specs, out_shape=out_shape,
        scratch_shapes=[pltpu.VMEM((N_SLABS, tm, LANES), F32),
                        pltpu.VMEM((N_SLABS, d4, tm // d4, LANES), F32)],
        compiler_params=_params(("arbitrary", "arbitrary")),
        name="prompt_front",
    )(x, mod.reshape(b, 1, N_MOD * d), lw["norm1_g"], lw["w_in"], lw["q_norm_g"], lw["k_norm_g"],
      lw["headmean"])


def _attn_kernel(q_ref, kp_ref, kc_ref, vp_ref, vc_ref, bias_ref, o_ref, st_ref, kbuf, vbuf,
                 *, q_rows):
    i = pl.program_id(2)
    qb = q_ref.shape[2]
    kbuf[0:BAND, :] = kp_ref[0, 0]
    kbuf[BAND:BAND + qb, :] = kc_ref[0, 0]
    vbuf[0:BAND, :] = vp_ref[0, 0]
    vbuf[BAND:BAND + qb, :] = vc_ref[0, 0]
    lane = lax.broadcasted_iota(jnp.int32, (q_rows, LANES), 1)
    low = lane < HEAD_DIM
    col = lax.broadcasted_iota(jnp.int32, (1, 2 * BAND), 1)
    first_mask = jnp.where((col < BAND) & (i == 0), -jnp.inf, 0.0)
    zero = jnp.zeros((), BF16)
    for j in range(qb // BAND):
        keys = slice(j * BAND, (j + 2) * BAND)
        for sub in range(BAND // q_rows):
            in_block = slice(sub * q_rows, (sub + 1) * q_rows)
            rows = slice(j * BAND + sub * q_rows, j * BAND + (sub + 1) * q_rows)
            stat = jnp.zeros((q_rows, LANES), F32)
            for p in range(N_SLABS):
                cols = slice(p * LANES, (p + 1) * LANES)
                q2 = q_ref[0, 0, rows, cols]
                k2 = kbuf[keys, cols]
                v2 = vbuf[keys, cols]
                halves = []
                for e in range(LANES // HEAD_DIM):
                    hd = p * (LANES // HEAD_DIM) + e
                    qm = jnp.where(low if e == 0 else jnp.logical_not(low), q2, zero)
                    s = (lax.dot_general(qm, k2, _NT, preferred_element_type=F32)
                         + bias_ref[hd, in_block, :])
                    if j == 0:
                        s = s + first_mask
                    m = jnp.max(s, axis=-1, keepdims=True)
                    ex = jnp.exp(s - m)
                    l = jnp.sum(ex, axis=-1, keepdims=True)
                    halves.append(_dot(ex.astype(BF16), v2) * (1.0 / l))
                    stat = jnp.where(lane == hd, m + jnp.log(l), stat)
                o_ref[0, 0, rows, cols] = jnp.where(low, halves[0], halves[1]).astype(BF16)
            st_ref[0, 0, rows, :] = stat


def _prompt_attention(qkv, bias, qb=512, q_rows=BAND):
    b, dil, u, _ = qkv.shape
    qb = min(qb, u)
    ratio = qb // BAND
    cur = lambda col: pl.BlockSpec((1, 1, qb, ATTN_W), lambda bi, r, i: (bi, r, i, col))
    prev = lambda col: pl.BlockSpec(
        (1, 1, BAND, ATTN_W), lambda bi, r, i: (bi, r, jnp.maximum(i * ratio - 1, 0), col))
    return pl.pallas_call(
        functools.partial(_attn_kernel, q_rows=q_rows),
        grid=(b, dil, u // qb),
        in_specs=[cur(0), prev(1), cur(1), prev(2), cur(2), _resident(bias.shape)],
        out_specs=[cur(0), pl.BlockSpec((1, 1, qb, LANES), lambda bi, r, i: (bi, r, i, 0))],
        out_shape=[jax.ShapeDtypeStruct((b, dil, u, ATTN_W), BF16),
                   jax.ShapeDtypeStruct((b, dil, u, LANES), F32)],
        scratch_shapes=[pltpu.VMEM((BAND + qb, ATTN_W), BF16), pltpu.VMEM((BAND + qb, ATTN_W), BF16)],
        compiler_params=_params(("arbitrary", "arbitrary", "arbitrary")),
        name=f"prompt_attn_d{dil}",
    )(qkv, qkv, qkv, qkv, qkv, bias)


def _back_kernel(x_ref, mod_ref, o1_ref, o4_ref, o16_ref, s1_ref, s4_ref, s16_ref,
                 glu_ref, halo_ref, cw_ref, cb_ref, lng_ref, lnb_ref, wpw_ref,
                 expand_ref, wo_ref, n2g_ref, w1_ref, w2_ref,
                 qbd_ref, kn_ref, vn_ref, ck_ref, cv_ref, tab_ref, hmask_ref,
                 y_ref, nk_ref, nv_ref, os_ref,
                 lnat, onat, gbuf, act, a_ref, kb, vb, *, row_chunk, ff_chunk, t_new, n_groups):
    tm = x_ref.shape[1]
    group = (pl.program_id(0) * pl.num_programs(1) + pl.program_id(1)) % n_groups
    _window_attend(qbd_ref[0, 0], lambda p: tab_ref[group, p],
                   kn_ref.at[0], vn_ref.at[0], ck_ref.at[0], cv_ref.at[0], hmask_ref,
                   nk_ref.at[0], nv_ref.at[0], os_ref.at[0, 0], kb, vb,
                   t_new=t_new, n_patterns=tab_ref.shape[1], row_chunk=row_chunk)

    def steps(first, count):
        return pl.ds(CONV_PITCH * first, count, stride=CONV_PITCH)

    n_cs = gbuf.shape[0]
    has_prev = (pl.program_id(1) > 0).astype(F32)
    for s, part in enumerate(_slabs(halo_ref[0] * has_prev)):
        gbuf[s, steps(0, HALO), :] = part
    for s, part in enumerate(_slabs(glu_ref[0])):
        gbuf[s, steps(HALO, tm), :] = part
    for which, (o_ref, s_ref) in enumerate(((o4_ref, s4_ref), (o16_ref, s16_ref))):
        dil = o_ref.shape[1]
        for r in range(dil):
            rows = pl.ds(r, tm // dil, stride=dil)
            lnat[which, rows, :] = s_ref[0, r]
            for s, part in enumerate(_slabs(o_ref[0, r])):
                onat[which, s, rows, :] = part.astype(F32)

    lead = HALO - CONV_HIST
    for c in range(tm // row_chunk):
        accs = [jnp.broadcast_to(b_s, (row_chunk, LANES)) for b_s in _slabs(cb_ref[...])]
        for j in range(CONV_K):
            for s in range(n_cs):
                accs[s] = accs[s] + (gbuf[s, steps(c * row_chunk + lead + j, row_chunk), :]
                                     * cw_ref[j:j + 1, s * LANES:(s + 1) * LANES])
        act[c * row_chunk:(c + 1) * row_chunk, :] = _conv_post(
            jnp.concatenate(accs, axis=1), lng_ref, lnb_ref)
    o_conv = _dot(act[...], wpw_ref[...]).astype(BF16)

    ls = [s1_ref[0, 0], lnat[0], lnat[1]]
    top = jnp.maximum(jnp.maximum(ls[0], ls[1]), ls[2])
    ps = [jnp.exp(t - top) for t in ls]
    inv = 1.0 / (ps[0] + ps[1] + ps[2])
    os = [o1_ref[0, 0].astype(F32)] + [
        jnp.concatenate([onat[which, s] for s in range(N_SLABS)], axis=1) for which in range(2)]
    o_attn = None
    for p_d, o_d in zip(ps, os):
        w = p_d * inv
        w_hi = w.astype(BF16)
        w_lo = (w - w_hi.astype(F32)).astype(BF16)
        w_full = _dot(w_hi, expand_ref[...]) + _dot(w_lo, expand_ref[...])
        term = w_full * o_d
        o_attn = term if o_attn is None else o_attn + term
    aw = o_attn.shape[1]
    mix = _dot(o_attn.astype(BF16), wo_ref[0:aw, :]) + _dot(o_conv, wo_ref[aw:, :])
    y_ref[0] = _mlp_tail(x_ref[0], mix, mod_ref[0], n2g_ref, w1_ref, w2_ref, a_ref, ff_chunk)


def _prompt_back(x, mod, outs, stats, glu, lw, sample, row_chunk=64, ff_chunk=1024):
    b, s, d = x.shape
    dff = lw["w_ff1"].shape[1]
    cw = glu.shape[2]
    qbd, k_new_t, v_new_t, cache_k_t, cache_v_t, tabs, hmask = sample
    nb, n_groups, rows_g, gw = qbd.shape
    n_hist = cache_k_t.shape[2]
    t_new = rows_g // (gw // HEAD_DIM)
    tm = (b * s) // (nb * n_groups)
    assert b * s == tm * nb * n_groups and s % tm == 0 and tm % row_chunk == 0
    n = s // tm
    unit = lambda bi, i: ((bi * n + i) // n_groups, (bi * n + i) % n_groups)
    hosted = lambda shape: pl.BlockSpec((1,) + shape, lambda bi, i: unit(bi, i) + (0,))
    hosted4 = lambda shape: pl.BlockSpec((1, 1) + shape, lambda bi, i: unit(bi, i) + (0, 0))
    tok = lambda width: pl.BlockSpec((1, tm, width), lambda bi, i: (bi, i, 0))
    plane = lambda a: pl.BlockSpec((1, a.shape[1], tm // a.shape[1], a.shape[3]),
                                   lambda bi, i: (bi, 0, i, 0))
    halo = pl.BlockSpec((1, HALO, cw), lambda bi, i: (bi, jnp.maximum(i * (tm // HALO) - 1, 0), 0))
    in_specs = ([tok(d), pl.BlockSpec((1, 1, N_MOD * d), lambda bi, i: (bi, 0, 0))]
                + [plane(a) for a in outs] + [plane(a) for a in stats]
                + [tok(cw), halo,
                   _resident((CONV_K, cw)), _resident((1, cw)), _resident((1, cw)), _resident((1, cw)),
                   _resident((cw, cw)),
                   _resident((LANES, ATTN_W)), _resident((d, d)), _resident((1, d)),
                   _resident((d, dff)), _resident((dff, d)),
                   hosted4((rows_g, gw)), hosted((gw, t_new)), hosted((gw, t_new)),
                   hosted((gw, n_hist)), hosted((gw, n_hist)),
                   _resident(tabs.shape), _resident(hmask.shape)])
    out_specs = [tok(d), hosted((gw, n_hist)), hosted((gw, n_hist)), hosted4((t_new, gw))]
    out_shape = [jax.ShapeDtypeStruct((b, s, d), F32),
                 jax.ShapeDtypeStruct(cache_k_t.shape, cache_k_t.dtype),
                 jax.ShapeDtypeStruct(cache_v_t.shape, cache_v_t.dtype),
                 jax.ShapeDtypeStruct((nb, n_groups, t_new, gw), F32)]
    return pl.pallas_call(
        functools.partial(_back_kernel, row_chunk=row_chunk, ff_chunk=ff_chunk,
                          t_new=t_new, n_groups=n_groups),
        grid=(b, n),
        in_specs=in_specs, out_specs=out_specs, out_shape=out_shape,
        scratch_shapes=[pltpu.VMEM((2, tm, LANES), F32),
                        pltpu.VMEM((2, N_SLABS, tm, LANES), F32),
                        pltpu.VMEM((cw // LANES, CONV_PITCH * (tm + HALO), LANES), F32),
                        pltpu.VMEM((tm, cw), BF16), pltpu.VMEM((tm, dff), BF16),
                        pltpu.VMEM((gw, n_hist + LANES), BF16), pltpu.VMEM((gw, n_hist + LANES), BF16)],
        compiler_params=_params(("arbitrary", "arbitrary")),
        name="prompt_back",
    )(x, mod.reshape(b, 1, N_MOD * d), *outs, *stats, glu, glu, lw["conv_dw"], lw["conv_dw_b"], lw["conv_ln_g"],
      lw["conv_ln_b"], lw["w_pw2"], lw["expand"], lw["w_o"], lw["norm2_g"], lw["w_ff1"], lw["w_ff2"],
      qbd, k_new_t, v_new_t, cache_k_t, cache_v_t, tabs, hmask)


def _sample_front_kernel(x_ref, mod_ref, hist_ref, n1g_ref, w_in_ref, qg_ref, kg_ref, headmean_ref,
                         cw_ref, cb_ref, lng_ref, lnb_ref, wpw_ref,
                         q_ref, k_ref, v_ref, glu_ref, oc_ref, hbuf, act, *, t_new):
    nb = mod_ref.shape[0]
    d = x_ref.shape[1]
    mod = mod_ref[...]
    for t in range(t_new):
        rows = slice(t * nb, (t + 1) * nb)
        h = _modulated_rms(x_ref[rows, :], n1g_ref[...], _mod_slice(mod, 1, d), _mod_slice(mod, 0, d))
        hbuf[rows, :] = h.astype(BF16)
    q, k, v, glu = _in_proj(hbuf[...], w_in_ref, qg_ref, kg_ref, headmean_ref)
    q_ref[...] = q
    k_ref[...] = k
    v_ref[...] = v
    glu_ref[...] = glu

    def slab(sidx):
        if sidx < CONV_HIST:
            return hist_ref[sidx]
        return glu_ref[(sidx - CONV_HIST) * nb:(sidx - CONV_HIST + 1) * nb, :]

    for t in range(t_new):
        acc = jnp.broadcast_to(cb_ref[...], (nb, glu.shape[1]))
        for j in range(CONV_K):
            acc = acc + slab(t + j) * cw_ref[j:j + 1, :]
        act[t * nb:(t + 1) * nb, :] = _conv_post(acc, lng_ref, lnb_ref)
    oc_ref[...] = _dot(act[...], wpw_ref[...]).astype(BF16)


def _sample_front(x_tm, mod, hist_tm, lw, t_new):
    rows, d = x_tm.shape
    cw = lw["conv_dw"].shape[1]
    full = lambda shape: pl.BlockSpec(shape, lambda i: (0,) * len(shape))
    args = (x_tm, mod, hist_tm, lw["norm1_g"], lw["w_in"], lw["q_norm_g"], lw["k_norm_g"],
            lw["headmean"], lw["conv_dw"], lw["conv_dw_b"], lw["conv_ln_g"], lw["conv_ln_b"],
            lw["w_pw2"])
    return pl.pallas_call(
        functools.partial(_sample_front_kernel, t_new=t_new),
        grid=(1,),
        in_specs=[full(a.shape) for a in args],
        out_specs=[full((rows, ATTN_W))] * 3 + [full((rows, cw)), full((rows, cw))],
        out_shape=[jax.ShapeDtypeStruct((rows, ATTN_W), F32)] * 3 + [
            jax.ShapeDtypeStruct((rows, cw), F32), jax.ShapeDtypeStruct((rows, cw), BF16)],
        scratch_shapes=[pltpu.VMEM((rows, d), BF16), pltpu.VMEM((rows, cw), BF16)],
        compiler_params=_params(("arbitrary",)),
        name="sample_front",
    )(*args)


def _window_attend(qbd, table, kn_ref, vn_ref, ck_ref, cv_ref, hmask_ref, nk_ref, nv_ref, o_ref,
                   kb, vb, *, t_new, n_patterns, row_chunk):
    width, n_hist = ck_ref.shape
    keep = n_hist - LANES
    lane = lax.broadcasted_iota(jnp.int32, (row_chunk, LANES), 1)
    is_new = lane >= LANES - t_new
    for src_ref, new_ref, dst_ref, buf in ((ck_ref, kn_ref, nk_ref, kb), (cv_ref, vn_ref, nv_ref, vb)):
        for c in range(0, width, row_chunk):
            rows = slice(c, c + row_chunk)
            old = src_ref[rows, :]
            new = jnp.concatenate([jnp.zeros((row_chunk, LANES - t_new), F32), new_ref[rows, :]], axis=1)
            buf[rows, 0:n_hist] = old.astype(BF16)
            buf[rows, n_hist:] = new.astype(BF16)
            shifted = pltpu.roll(old, n_hist - t_new, axis=1)
            dst_ref[rows, 0:keep] = shifted[:, 0:keep]
            dst_ref[rows, keep:] = jnp.where(is_new, new, shifted[:, keep:])

    s = _dot(qbd, kb[...])
    ms, ls, exs = [], [], []
    for p in range(n_patterns):
        sp = s + table(p)
        m = jnp.max(sp, axis=-1, keepdims=True)
        ex = jnp.exp(sp - m)
        ms.append(m)
        ls.append(jnp.sum(ex, axis=-1, keepdims=True))
        exs.append(ex.astype(BF16))
    pv = lax.dot_general(jnp.concatenate(exs, axis=0), vb[...], _NT, preferred_element_type=F32)
    nr = s.shape[0]
    accs = [pv[p * nr:(p + 1) * nr, :] for p in range(n_patterns)]
    top = functools.reduce(jnp.maximum, ms)
    ws = [jnp.exp(m - top) for m in ms]
    num = sum(w * a for w, a in zip(ws, accs))
    den = sum(w * l for w, l in zip(ws, ls))
    rows = num * (1.0 / den) * hmask_ref[...]
    heads = nr // t_new
    for t in range(t_new):
        o_ref[t:t + 1, :] = jnp.sum(rows[t * heads:(t + 1) * heads, :], axis=0, keepdims=True)


def _sample_back_kernel(x_ref, mod_ref, oa_ref, oc_ref, wo_ref, n2g_ref, w1_ref, w2_ref,
                        y_ref, a_ref, *, ff_chunk):
    aw = oa_ref.shape[1]
    mix = _dot(oa_ref[...].astype(BF16), wo_ref[0:aw, :]) + _dot(oc_ref[...], wo_ref[aw:, :])
    y_ref[...] = _mlp_tail(x_ref[...], mix, mod_ref[...], n2g_ref, w1_ref, w2_ref, a_ref, ff_chunk)


def _sample_back(x_tm, mod, o_attn_tm, o_conv_tm, lw, t_new, ff_chunk=512):
    rows, d = x_tm.shape
    nb = rows // t_new
    dff = lw["w_ff1"].shape[1]
    cw = o_conv_tm.shape[1]
    tok = lambda width: pl.BlockSpec((nb, width), lambda t: (t, 0))
    return pl.pallas_call(
        functools.partial(_sample_back_kernel, ff_chunk=ff_chunk),
        grid=(t_new,),
        in_specs=[tok(d), _resident(mod.shape), tok(ATTN_W), tok(cw),
                  _resident((d, d)), _resident((1, d)), _resident((d, dff)), _resident((dff, d))],
        out_specs=tok(d),
        out_shape=jax.ShapeDtypeStruct((rows, d), F32),
        scratch_shapes=[pltpu.VMEM((nb, dff), BF16)],
        compiler_params=_params(("arbitrary",)),
        name="sample_back",
    )(x_tm, mod, o_attn_tm, o_conv_tm, lw["w_o"], lw["norm2_g"], lw["w_ff1"], lw["w_ff2"])


def _layer_weights(l, norm1_g, norm2_g, w_in, q_norm_g, k_norm_g, conv_dw, conv_dw_b,
                   conv_ln_g, conv_ln_b, w_pw2, w_o, w_ff1, w_ff2):
    head_of = jnp.arange(ATTN_W) // HEAD_DIM
    headmean = (head_of[:, None] == head_of[None, :]).astype(F32) / HEAD_DIM
    expand = (jnp.arange(LANES)[:, None] == head_of[None, :]).astype(BF16)
    row = lambda a: a[l].reshape(1, -1)
    return {
        "norm1_g": row(norm1_g), "norm2_g": row(norm2_g),
        "w_in": w_in[l].astype(BF16),
        "q_norm_g": jnp.tile(q_norm_g[l], N_HEADS).reshape(1, ATTN_W),
        "k_norm_g": jnp.tile(k_norm_g[l], N_HEADS).reshape(1, ATTN_W),
        "headmean": headmean.astype(BF16), "expand": expand,
        "conv_dw": conv_dw[l], "conv_dw_b": row(conv_dw_b),
        "conv_ln_g": row(conv_ln_g), "conv_ln_b": row(conv_ln_b),
        "w_pw2": w_pw2[l].astype(BF16), "w_o": w_o[l].astype(BF16),
        "w_ff1": w_ff1[l].astype(BF16), "w_ff2": w_ff2[l].astype(BF16),
    }


def _feature_major(cache):
    nb, n_hist = cache.shape[0], cache.shape[1]
    return cache.transpose(0, 2, 3, 1).reshape(nb, ATTN_W, n_hist)


def _position_major(cache_t):
    nb, _, n_hist = cache_t.shape
    return cache_t.reshape(nb, N_HEADS, HEAD_DIM, n_hist).transpose(0, 3, 1, 2)


def kernel(x_prompt, x_sample, cache_k, cache_v, state_conv, c_prompt, c_sample, rel_bias, norm1_g, norm2_g, w_ada, b_ada, w_in, q_norm_g, k_norm_g, conv_dw, conv_dw_b, conv_ln_g, conv_ln_b, w_pw2, w_o, w_ff1, w_ff2):
    depth = w_in.shape[0]
    b, s, d = x_prompt.shape
    nb, t_new, _ = x_sample.shape
    n_hist = cache_k.shape[2]
    assert s % WINDOW_MAX == 0 and n_hist % LANES == 0 and t_new <= LANES

    prompt_bias = _prompt_bias_tables(rel_bias)
    n_groups = SAMPLE_HEAD_GROUPS
    hpg = N_HEADS // n_groups
    gw = hpg * HEAD_DIM
    n_keys = n_hist + LANES
    sample_tabs = _sample_bias_tables(rel_bias, n_hist, t_new)
    sample_tabs = sample_tabs.reshape(len(PATTERNS), t_new, n_groups, hpg, n_keys).transpose(
        2, 0, 1, 3, 4).reshape(n_groups, len(PATTERNS), t_new * hpg, n_keys)
    hmask_g = (jnp.arange(hpg)[:, None] == (jnp.arange(gw) // HEAD_DIM)[None, :]).astype(F32)
    hmask = jnp.tile(hmask_g, (t_new, 1))

    xp = x_prompt
    xs_tm = x_sample.transpose(1, 0, 2).reshape(t_new * nb, d)
    c_all = jnp.concatenate([c_prompt, c_sample], axis=0)
    kp_l, vp_l, cp_l, ks_l, vs_l, cs_l = [], [], [], [], [], []
    for l in range(depth):
        lw = _layer_weights(l, norm1_g, norm2_g, w_in, q_norm_g, k_norm_g, conv_dw, conv_dw_b,
                            conv_ln_g, conv_ln_b, w_pw2, w_o, w_ff1, w_ff2)
        mod = _modulation(c_all, w_ada[l], b_ada[l])
        mod_p, mod_s = mod[:b], mod[b:]

        hist_tm = state_conv[l].transpose(1, 0, 2)
        q_s, k_s, v_s, glu_s, oc_s = _sample_front(xs_tm, mod_s, hist_tm, lw, t_new)
        to_bm = lambda a: a.reshape(t_new, nb, -1).transpose(1, 0, 2)
        q_bm = (to_bm(q_s) * (HEAD_DIM ** -0.5)).reshape(nb, t_new, n_groups, gw).transpose(0, 2, 1, 3)
        qbd = (q_bm[:, :, :, None, :] * hmask_g[None, None, None]).reshape(nb, n_groups, t_new * hpg, gw)
        new_t = lambda a: a.reshape(t_new, nb, ATTN_W).transpose(1, 2, 0)
        sample = (qbd.astype(BF16), new_t(k_s), new_t(v_s),
                  _feature_major(cache_k[l]), _feature_major(cache_v[l]), sample_tabs, hmask)

        qkv1, qkv4, qkv16, glu, k_tail, v_tail = _prompt_front(xp, mod_p, lw)
        outs, stats = [], []
        for pi, qkv in enumerate((qkv1.reshape(b, 1, s, -1), qkv4, qkv16)):
            o_d, st_d = _prompt_attention(qkv, prompt_bias[pi])
            outs.append(o_d)
            stats.append(st_d)
        xp, new_k_t, new_v_t, o_s = _prompt_back(xp, mod_p, outs, stats, glu, lw, sample)
        kp_l.append(k_tail.reshape(b, -1, N_HEADS, HEAD_DIM))
        vp_l.append(v_tail.reshape(b, -1, N_HEADS, HEAD_DIM))
        cp_l.append(glu[:, s - CONV_HIST:, :])

        o_s_tm = o_s.transpose(2, 0, 1, 3).reshape(t_new * nb, ATTN_W)
        xs_tm = _sample_back(xs_tm, mod_s, o_s_tm, oc_s, lw, t_new)
        ks_l.append(_position_major(new_k_t))
        vs_l.append(_position_major(new_v_t))
        cs_l.append(jnp.concatenate([state_conv[l].astype(F32), to_bm(glu_s)], axis=1)[:, -CONV_HIST:])

    xs = xs_tm.reshape(t_new, nb, d).transpose(1, 0, 2)
    return (xp, xs, jnp.stack(kp_l), jnp.stack(vp_l), jnp.stack(cp_l),
            jnp.stack(ks_l), jnp.stack(vs_l), jnp.stack(cs_l))
```

```python
import functools
import math

import jax
import jax.numpy as jnp
from jax import lax
from jax.experimental import pallas as pl
from jax.experimental.pallas import tpu as pltpu

F32 = jnp.float32
BF16 = jnp.bfloat16

HEAD_DIM = 64
N_HEADS = 8
ATTN_W = N_HEADS * HEAD_DIM
CONV_K = 31
CONV_HIST = CONV_K - 1
PATTERNS = ((128, 1), (512, 4), (2048, 16))
WINDOW_MAX = 2048
BAND = 128
NUM_BUCKETS = 32
EPS = 1e-6
N_MOD = 6
LANES = 128
SUBLANES = 8
HALO = 32
CONV_PITCH = 3
VMEM_LIMIT = 56 * 1024 * 1024
N_SLABS = ATTN_W // LANES
SAMPLE_HEAD_GROUPS = 2

_NT = (((1,), (1,)), ((), ()))


def _dot(a, b):
    return jnp.dot(a, b, preferred_element_type=F32)


def _resident(shape):
    return pl.BlockSpec(shape, lambda *_: (0,) * len(shape), pipeline_mode=pl.Buffered(1))


def _params(sem, flags=None):
    return pltpu.CompilerParams(dimension_semantics=sem, vmem_limit_bytes=VMEM_LIMIT, flags=flags)


def _mod_kernel(c_ref, w_ref, b_ref, o_ref):
    c = c_ref[...]
    a = c * jax.nn.sigmoid(c)
    w = w_ref[...]
    a_hi = a.astype(BF16)
    a_lo = (a - a_hi.astype(F32)).astype(BF16)
    w_hi = w.astype(BF16)
    w_lo = (w - w_hi.astype(F32)).astype(BF16)
    o_ref[...] = _dot(a_hi, w_hi) + _dot(a_lo, w_hi) + _dot(a_hi, w_lo) + b_ref[...]


def _modulation(c_all, w_ada, b_ada):
    rows, d = c_all.shape
    n = w_ada.shape[1]
    tn = d
    return pl.pallas_call(
        _mod_kernel,
        grid=(n // tn,),
        in_specs=[pl.BlockSpec((rows, d), lambda j: (0, 0)),
                  pl.BlockSpec((d, tn), lambda j: (0, j)),
                  pl.BlockSpec((1, tn), lambda j: (0, j))],
        out_specs=pl.BlockSpec((rows, tn), lambda j: (0, j)),
        out_shape=jax.ShapeDtypeStruct((rows, n), F32),
        compiler_params=_params(("arbitrary",)),
        name="mod",
    )(c_all, w_ada, b_ada.reshape(1, n))


def _rel_bucket(dist):
    max_exact = NUM_BUCKETS // 2
    df = jnp.maximum(dist, 1).astype(F32)
    large = max_exact + (jnp.log(df / max_exact) / math.log(WINDOW_MAX / max_exact)
                         * (NUM_BUCKETS - max_exact)).astype(jnp.int32)
    large = jnp.minimum(large, NUM_BUCKETS - 1)
    return jnp.where(dist < max_exact, dist, large)


def _prompt_bias_kernel(rel_ref, bkt_ref, o_ref):
    bkt = bkt_ref[0]
    for h in range(o_ref.shape[1]):
        acc = jnp.full(bkt.shape, -jnp.inf, F32)
        for b in range(NUM_BUCKETS):
            acc = jnp.where(bkt == b, rel_ref[b, h], acc)
        o_ref[0, h] = acc


def _prompt_bias_tables(rel_bias):
    i = jnp.arange(BAND)[:, None]
    jj = jnp.arange(2 * BAND)[None, :]
    delta = BAND + i - jj
    valid = (delta >= 0) & (delta <= BAND)
    bkt = jnp.stack([jnp.where(valid, _rel_bucket(jnp.maximum(delta, 0) * d), -1)
                     for (_, d) in PATTERNS]).astype(jnp.int32)
    npat = len(PATTERNS)
    return pl.pallas_call(
        _prompt_bias_kernel,
        grid=(npat,),
        in_specs=[pl.BlockSpec(memory_space=pltpu.SMEM),
                  pl.BlockSpec((1, BAND, 2 * BAND), lambda p: (p, 0, 0))],
        out_specs=pl.BlockSpec((1, N_HEADS, BAND, 2 * BAND), lambda p: (p, 0, 0, 0)),
        out_shape=jax.ShapeDtypeStruct((npat, N_HEADS, BAND, 2 * BAND), F32),
        compiler_params=_params(("arbitrary",)),
        name="prompt_bias",
    )(rel_bias, bkt)


def _sample_bias_kernel(relrow_ref, bkt_ref, o_ref):
    bkt = bkt_ref[0]
    reps = bkt.shape[1] // LANES
    acc = jnp.full(bkt.shape, -jnp.inf, F32)
    for b in range(NUM_BUCKETS):
        acc = jnp.where(bkt == b, jnp.tile(relrow_ref[b], (1, reps)), acc)
    o_ref[0] = acc


def _sample_bias_tables(rel_bias, n_hist, t_new):
    n_keys = n_hist + LANES
    c = jnp.arange(n_keys)[None, :]
    t = jnp.arange(t_new)[:, None]
    key = jnp.where(c < n_hist, c, c - (LANES - t_new))
    real = (c < n_hist) | (c >= n_keys - t_new)
    dist = n_hist + t - key
    tabs = []
    for (w, d) in PATTERNS:
        valid = real & (dist >= 0) & (dist <= w) & (dist % d == 0)
        tabs.append(jnp.where(valid, _rel_bucket(jnp.maximum(dist, 0)), -1))
    bkt = jnp.stack(tabs).astype(jnp.int32)
    bkt = jnp.repeat(bkt, N_HEADS, axis=1)
    relrow = jnp.broadcast_to(jnp.tile(rel_bias, (1, t_new))[:, :, None],
                              (NUM_BUCKETS, t_new * N_HEADS, LANES))
    npat = len(PATTERNS)
    rows = t_new * N_HEADS
    return pl.pallas_call(
        _sample_bias_kernel,
        grid=(npat,),
        in_specs=[pl.BlockSpec((NUM_BUCKETS, rows, LANES), lambda p: (0, 0, 0)),
                  pl.BlockSpec((1, rows, n_keys), lambda p: (p, 0, 0))],
        out_specs=pl.BlockSpec((1, rows, n_keys), lambda p: (p, 0, 0)),
        out_shape=jax.ShapeDtypeStruct((npat, rows, n_keys), F32),
        compiler_params=_params(("arbitrary",)),
        name="sample_bias",
    )(relrow, bkt)


def _modulated_rms(x, g, scale, shift):
    y = x * lax.rsqrt(jnp.mean(x * x, axis=-1, keepdims=True) + EPS) * g
    return y * (1.0 + scale) + shift


def _mod_slice(mod, k, d):
    return mod[:, k * d:(k + 1) * d]


def _head_norm(t, g, headmean_ref):
    ms = _dot((t * t).astype(BF16), headmean_ref[...])
    return t * lax.rsqrt(ms + EPS) * g


def _in_proj(hb, w_in_ref, qg_ref, kg_ref, headmean_ref):
    aw = ATTN_W
    q = _head_norm(_dot(hb, w_in_ref[:, 0:aw]), qg_ref[...], headmean_ref)
    k = _head_norm(_dot(hb, w_in_ref[:, aw:2 * aw]), kg_ref[...], headmean_ref)
    v = _dot(hb, w_in_ref[:, 2 * aw:3 * aw])
    cw = (w_in_ref.shape[1] - 3 * aw) // 2
    ua = _dot(hb, w_in_ref[:, 3 * aw:3 * aw + cw])
    ub = _dot(hb, w_in_ref[:, 3 * aw + cw:3 * aw + 2 * cw])
    return q, k, v, ua * jax.nn.sigmoid(ub)


def _conv_post(dw, lng_ref, lnb_ref):
    mu = jnp.mean(dw, axis=-1, keepdims=True)
    xc = dw - mu
    var = jnp.mean(xc * xc, axis=-1, keepdims=True)
    y = xc * lax.rsqrt(var + EPS) * lng_ref[...] + lnb_ref[...]
    return (y * jax.nn.sigmoid(y)).astype(BF16)


def _mlp_tail(x, mix, mod, n2g_ref, w1_ref, w2_ref, a_ref, ff_chunk):
    d = x.shape[-1]
    x1 = x + _mod_slice(mod, 2, d) * mix
    h2 = _modulated_rms(x1, n2g_ref[...], _mod_slice(mod, 4, d), _mod_slice(mod, 3, d)).astype(BF16)
    for c in range(w1_ref.shape[1] // ff_chunk):
        a = jnp.maximum(_dot(h2, w1_ref[:, c * ff_chunk:(c + 1) * ff_chunk]), 0.0)
        a_ref[:, c * ff_chunk:(c + 1) * ff_chunk] = (a * a).astype(BF16)
    return x1 + _mod_slice(mod, 5, d) * _dot(a_ref[...], w2_ref[...])


def _slabs(x):
    return [x[:, s * LANES:(s + 1) * LANES] for s in range(x.shape[1] // LANES)]


def _regroup_store(x, nat, p4, out1_ref, out4_ref, out16_ref, which):
    tm = x.shape[0]
    d4 = PATTERNS[1][1]
    d16 = PATTERNS[2][1]
    sub = d16 // d4
    cols = slice(which * ATTN_W, (which + 1) * ATTN_W)
    out1_ref[0, :, cols] = x.astype(BF16)
    for s, xs in enumerate(_slabs(x)):
        nat[s] = xs
    for r4 in range(d4):
        parts = []
        for s in range(N_SLABS):
            part = nat[s, pl.ds(r4, tm // d4, stride=d4), :]
            p4[s, r4] = part
            parts.append(part)
        out4_ref[0, r4, :, cols] = jnp.concatenate(parts, axis=1).astype(BF16)
    for r16 in range(d16):
        r4, a = r16 % d4, r16 // d4
        parts = [p4[s, r4, pl.ds(a, tm // d16, stride=sub), :] for s in range(N_SLABS)]
        out16_ref[0, r16, :, cols] = jnp.concatenate(parts, axis=1).astype(BF16)


def _front_kernel(x_ref, mod_ref, n1g_ref, w_in_ref, qg_ref, kg_ref, headmean_ref,
                  qkv1_ref, qkv4_ref, qkv16_ref, glu_ref, kf_ref, vf_ref, nat, p4):
    d = x_ref.shape[2]
    mod = mod_ref[0]
    h = _modulated_rms(x_ref[0], n1g_ref[...], _mod_slice(mod, 1, d), _mod_slice(mod, 0, d))
    q, k, v, glu = _in_proj(h.astype(BF16), w_in_ref, qg_ref, kg_ref, headmean_ref)
    glu_ref[0] = glu
    kf_ref[0] = k
    vf_ref[0] = v
    for which, t in enumerate((q * (HEAD_DIM ** -0.5), k, v)):
        _regroup_store(t, nat, p4, qkv1_ref, qkv4_ref, qkv16_ref, which)


def _prompt_front(x, mod, lw, tm=1024):
    b, s, d = x.shape
    win = min(WINDOW_MAX, s)
    tail_start = (s - win) // tm
    cw = (lw["w_in"].shape[1] - 3 * ATTN_W) // 2
    d4, d16 = PATTERNS[1][1], PATTERNS[2][1]
    tok = lambda width: pl.BlockSpec((1, tm, width), lambda bi, i: (bi, i, 0))
    qkv_w = 3 * ATTN_W
    plane = lambda dil: pl.BlockSpec((1, dil, tm // dil, qkv_w), lambda bi, i: (bi, 0, i, 0))
    tail = pl.BlockSpec((1, tm, ATTN_W), lambda bi, i: (bi, jnp.maximum(i - tail_start, 0), 0))
    in_specs = [tok(d),
                pl.BlockSpec((1, 1, N_MOD * d), lambda bi, i: (bi, 0, 0)),
                _resident((1, d)), _resident(lw["w_in"].shape), _resident((1, ATTN_W)),
                _resident((1, ATTN_W)), _resident((ATTN_W, ATTN_W))]
    out_specs = [tok(qkv_w), plane(d4), plane(d16), tok(cw), tail, tail]
    plane_shape = lambda dil: jax.ShapeDtypeStruct((b, dil, s // dil, qkv_w), BF16)
    out_shape = ([jax.ShapeDtypeStruct((b, s, qkv_w), BF16), plane_shape(d4), plane_shape(d16)] + [
        jax.ShapeDtypeStruct((b, s, cw), F32),
        jax.ShapeDtypeStruct((b, win, ATTN_W), F32), jax.ShapeDtypeStruct((b, win, ATTN_W), F32)])
    return pl.pallas_call(
        _front_kernel,
        grid=(b, s // tm),
        in_specs=in_specs, out_specs=out_specs, out_shape=out_shape,
        scratch_shapes=[pltpu.VMEM((N_SLABS, tm, LANES), F32),
                        pltpu.VMEM((N_SLABS, d4, tm // d4, LANES), F32)],
        compiler_params=_params(("arbitrary", "arbitrary")),
        name="prompt_front",
    )(x, mod.reshape(b, 1, N_MOD * d), lw["norm1_g"], lw["w_in"], lw["q_norm_g"], lw["k_norm_g"],
      lw["headmean"])


def _attn_kernel(q_ref, kp_ref, kc_ref, vp_ref, vc_ref, bias_ref, o_ref, st_ref, kbuf, vbuf,
                 *, q_rows):
    i = pl.program_id(2)
    qb = q_ref.shape[2]
    kbuf[0:BAND, :] = kp_ref[0, 0]
    kbuf[BAND:BAND + qb, :] = kc_ref[0, 0]
    vbuf[0:BAND, :] = vp_ref[0, 0]
    vbuf[BAND:BAND + qb, :] = vc_ref[0, 0]
    lane = lax.broadcasted_iota(jnp.int32, (q_rows, LANES), 1)
    low = lane < HEAD_DIM
    col = lax.broadcasted_iota(jnp.int32, (1, 2 * BAND), 1)
    first_mask = jnp.where((col < BAND) & (i == 0), -jnp.inf, 0.0)
    zero = jnp.zeros((), BF16)
    for j in range(qb // BAND):
        keys = slice(j * BAND, (j + 2) * BAND)
        for sub in range(BAND // q_rows):
            in_block = slice(sub * q_rows, (sub + 1) * q_rows)
            rows = slice(j * BAND + sub * q_rows, j * BAND + (sub + 1) * q_rows)
            stat = jnp.zeros((q_rows, LANES), F32)
            for p in range(N_SLABS):
                cols = slice(p * LANES, (p + 1) * LANES)
                q2 = q_ref[0, 0, rows, cols]
                k2 = kbuf[keys, cols]
                v2 = vbuf[keys, cols]
                halves = []
                for e in range(LANES // HEAD_DIM):
                    hd = p * (LANES // HEAD_DIM) + e
                    qm = jnp.where(low if e == 0 else jnp.logical_not(low), q2, zero)
                    s = (lax.dot_general(qm, k2, _NT, preferred_element_type=F32)
                         + bias_ref[hd, in_block, :])
                    if j == 0:
                        s = s + first_mask
                    m = jnp.max(s, axis=-1, keepdims=True)
                    ex = jnp.exp(s - m)
                    l = jnp.sum(ex, axis=-1, keepdims=True)
                    halves.append(_dot(ex.astype(BF16), v2) * (1.0 / l))
                    stat = jnp.where(lane == hd, m + jnp.log(l), stat)
                o_ref[0, 0, rows, cols] = jnp.where(low, halves[0], halves[1]).astype(BF16)
            st_ref[0, 0, rows, :] = stat


def _prompt_attention(qkv, bias, qb=512, q_rows=BAND):
    b, dil, u, _ = qkv.shape
    qb = min(qb, u)
    ratio = qb // BAND
    cur = lambda col: pl.BlockSpec((1, 1, qb, ATTN_W), lambda bi, r, i: (bi, r, i, col))
    prev = lambda col: pl.BlockSpec(
        (1, 1, BAND, ATTN_W), lambda bi, r, i: (bi, r, jnp.maximum(i * ratio - 1, 0), col))
    return pl.pallas_call(
        functools.partial(_attn_kernel, q_rows=q_rows),
        grid=(b, dil, u // qb),
        in_specs=[cur(0), prev(1), cur(1), prev(2), cur(2), _resident(bias.shape)],
        out_specs=[cur(0), pl.BlockSpec((1, 1, qb, LANES), lambda bi, r, i: (bi, r, i, 0))],
        out_shape=[jax.ShapeDtypeStruct((b, dil, u, ATTN_W), BF16),
                   jax.ShapeDtypeStruct((b, dil, u, LANES), F32)],
        scratch_shapes=[pltpu.VMEM((BAND + qb, ATTN_W), BF16), pltpu.VMEM((BAND + qb, ATTN_W), BF16)],
        compiler_params=_params(("arbitrary", "arbitrary", "arbitrary")),
        name=f"prompt_attn_d{dil}",
    )(qkv, qkv, qkv, qkv, qkv, bias)


def _back_kernel(x_ref, mod_ref, o1_ref, o4_ref, o16_ref, s1_ref, s4_ref, s16_ref,
                 glu_ref, halo_ref, cw_ref, cb_ref, lng_ref, lnb_ref, wpw_ref,
                 expand_ref, wo_ref, n2g_ref, w1_ref, w2_ref,
                 qbd_ref, kn_ref, vn_ref, ck_ref, cv_ref, tab_ref, hmask_ref,
                 y_ref, nk_ref, nv_ref, os_ref,
                 lnat, onat, gbuf, act, a_ref, kb, vb, *, row_chunk, ff_chunk, t_new, n_groups):
    tm = x_ref.shape[1]
    group = (pl.program_id(0) * pl.num_programs(1) + pl.program_id(1)) % n_groups
    _window_attend(qbd_ref[0, 0], lambda p: tab_ref[group, p],
                   kn_ref.at[0], vn_ref.at[0], ck_ref.at[0], cv_ref.at[0], hmask_ref,
                   nk_ref.at[0], nv_ref.at[0], os_ref.at[0, 0], kb, vb,
                   t_new=t_new, n_patterns=tab_ref.shape[1], row_chunk=row_chunk)

    def steps(first, count):
        return pl.ds(CONV_PITCH * first, count, stride=CONV_PITCH)

    n_cs = gbuf.shape[0]
    has_prev = (pl.program_id(1) > 0).astype(F32)
    for s, part in enumerate(_slabs(halo_ref[0] * has_prev)):
        gbuf[s, steps(0, HALO), :] = part
    for s, part in enumerate(_slabs(glu_ref[0])):
        gbuf[s, steps(HALO, tm), :] = part
    for which, (o_ref, s_ref) in enumerate(((o4_ref, s4_ref), (o16_ref, s16_ref))):
        dil = o_ref.shape[1]
        for r in range(dil):
            rows = pl.ds(r, tm // dil, stride=dil)
            lnat[which, rows, :] = s_ref[0, r]
            for s, part in enumerate(_slabs(o_ref[0, r])):
                onat[which, s, rows, :] = part.astype(F32)

    lead = HALO - CONV_HIST
    for c in range(tm // row_chunk):
        accs = [jnp.broadcast_to(b_s, (row_chunk, LANES)) for b_s in _slabs(cb_ref[...])]
        for j in range(CONV_K):
            for s in range(n_cs):
                accs[s] = accs[s] + (gbuf[s, steps(c * row_chunk + lead + j, row_chunk), :]
                                     * cw_ref[j:j + 1, s * LANES:(s + 1) * LANES])
        act[c * row_chunk:(c + 1) * row_chunk, :] = _conv_post(
            jnp.concatenate(accs, axis=1), lng_ref, lnb_ref)
    o_conv = _dot(act[...], wpw_ref[...]).astype(BF16)

    ls = [s1_ref[0, 0], lnat[0], lnat[1]]
    top = jnp.maximum(jnp.maximum(ls[0], ls[1]), ls[2])
    ps = [jnp.exp(t - top) for t in ls]
    inv = 1.0 / (ps[0] + ps[1] + ps[2])
    os = [o1_ref[0, 0].astype(F32)] + [
        jnp.concatenate([onat[which, s] for s in range(N_SLABS)], axis=1) for which in range(2)]
    o_attn = None
    for p_d, o_d in zip(ps, os):
        w = p_d * inv
        w_hi = w.astype(BF16)
        w_lo = (w - w_hi.astype(F32)).astype(BF16)
        w_full = _dot(jnp.concatenate([w_hi, w_lo], axis=1), expand_ref[...])
        term = w_full * o_d
        o_attn = term if o_attn is None else o_attn + term
    aw = o_attn.shape[1]
    mix = _dot(o_attn.astype(BF16), wo_ref[0:aw, :]) + _dot(o_conv, wo_ref[aw:, :])
    y_ref[0] = _mlp_tail(x_ref[0], mix, mod_ref[0], n2g_ref, w1_ref, w2_ref, a_ref, ff_chunk)


def _prompt_back(x, mod, outs, stats, glu, lw, sample, row_chunk=64, ff_chunk=1024):
    b, s, d = x.shape
    dff = lw["w_ff1"].shape[1]
    cw = glu.shape[2]
    qbd, k_new_t, v_new_t, cache_k_t, cache_v_t, tabs, hmask = sample
    nb, n_groups, rows_g, gw = qbd.shape
    n_hist = cache_k_t.shape[2]
    t_new = rows_g // (gw // HEAD_DIM)
    tm = (b * s) // (nb * n_groups)
    assert b * s == tm * nb * n_groups and s % tm == 0 and tm % row_chunk == 0
    n = s // tm
    unit = lambda bi, i: ((bi * n + i) // n_groups, (bi * n + i) % n_groups)
    hosted = lambda shape: pl.BlockSpec((1,) + shape, lambda bi, i: unit(bi, i) + (0,))
    hosted4 = lambda shape: pl.BlockSpec((1, 1) + shape, lambda bi, i: unit(bi, i) + (0, 0))
    tok = lambda width: pl.BlockSpec((1, tm, width), lambda bi, i: (bi, i, 0))
    plane = lambda a: pl.BlockSpec((1, a.shape[1], tm // a.shape[1], a.shape[3]),
                                   lambda bi, i: (bi, 0, i, 0))
    halo = pl.BlockSpec((1, HALO, cw), lambda bi, i: (bi, jnp.maximum(i * (tm // HALO) - 1, 0), 0))
    in_specs = ([tok(d), pl.BlockSpec((1, 1, N_MOD * d), lambda bi, i: (bi, 0, 0))]
                + [plane(a) for a in outs] + [plane(a) for a in stats]
                + [tok(cw), halo,
                   _resident((CONV_K, cw)), _resident((1, cw)), _resident((1, cw)), _resident((1, cw)),
                   _resident((cw, cw)),
                   _resident(lw["expand"].shape), _resident((d, d)), _resident((1, d)),
                   _resident((d, dff)), _resident((dff, d)),
                   hosted4((rows_g, gw)), hosted((gw, t_new)), hosted((gw, t_new)),
                   hosted((gw, n_hist)), hosted((gw, n_hist)),
                   _resident(tabs.shape), _resident(hmask.shape)])
    out_specs = [tok(d), hosted((gw, n_hist)), hosted((gw, n_hist)), hosted4((t_new, gw))]
    out_shape = [jax.ShapeDtypeStruct((b, s, d), F32),
                 jax.ShapeDtypeStruct(cache_k_t.shape, cache_k_t.dtype),
                 jax.ShapeDtypeStruct(cache_v_t.shape, cache_v_t.dtype),
                 jax.ShapeDtypeStruct((nb, n_groups, t_new, gw), F32)]
    return pl.pallas_call(
        functools.partial(_back_kernel, row_chunk=row_chunk, ff_chunk=ff_chunk,
                          t_new=t_new, n_groups=n_groups),
        grid=(b, n),
        in_specs=in_specs, out_specs=out_specs, out_shape=out_shape,
        scratch_shapes=[pltpu.VMEM((2, tm, LANES), F32),
                        pltpu.VMEM((2, N_SLABS, tm, LANES), F32),
                        pltpu.VMEM((cw // LANES, CONV_PITCH * (tm + HALO), LANES), F32),
                        pltpu.VMEM((tm, cw), BF16), pltpu.VMEM((tm, dff), BF16),
                        pltpu.VMEM((gw, n_hist + LANES), BF16), pltpu.VMEM((gw, n_hist + LANES), BF16)],
        compiler_params=_params(("arbitrary", "arbitrary")),
        name="prompt_back",
    )(x, mod.reshape(b, 1, N_MOD * d), *outs, *stats, glu, glu, lw["conv_dw"], lw["conv_dw_b"], lw["conv_ln_g"],
      lw["conv_ln_b"], lw["w_pw2"], lw["expand"], lw["w_o"], lw["norm2_g"], lw["w_ff1"], lw["w_ff2"],
      qbd, k_new_t, v_new_t, cache_k_t, cache_v_t, tabs, hmask)


def _sample_front_kernel(x_ref, mod_ref, hist_ref, n1g_ref, w_in_ref, qg_ref, kg_ref, headmean_ref,
                         cw_ref, cb_ref, lng_ref, lnb_ref, wpw_ref,
                         q_ref, k_ref, v_ref, glu_ref, oc_ref, hbuf, act, *, t_new):
    nb = mod_ref.shape[0]
    d = x_ref.shape[1]
    mod = mod_ref[...]
    for t in range(t_new):
        rows = slice(t * nb, (t + 1) * nb)
        h = _modulated_rms(x_ref[rows, :], n1g_ref[...], _mod_slice(mod, 1, d), _mod_slice(mod, 0, d))
        hbuf[rows, :] = h.astype(BF16)
    q, k, v, glu = _in_proj(hbuf[...], w_in_ref, qg_ref, kg_ref, headmean_ref)
    q_ref[...] = q
    k_ref[...] = k
    v_ref[...] = v
    glu_ref[...] = glu

    def slab(sidx):
        if sidx < CONV_HIST:
            return hist_ref[sidx]
        return glu_ref[(sidx - CONV_HIST) * nb:(sidx - CONV_HIST + 1) * nb, :]

    for t in range(t_new):
        acc = jnp.broadcast_to(cb_ref[...], (nb, glu.shape[1]))
        for j in range(CONV_K):
            acc = acc + slab(t + j) * cw_ref[j:j + 1, :]
        act[t * nb:(t + 1) * nb, :] = _conv_post(acc, lng_ref, lnb_ref)
    oc_ref[...] = _dot(act[...], wpw_ref[...]).astype(BF16)


def _sample_front(x_tm, mod, hist_tm, lw, t_new):
    rows, d = x_tm.shape
    cw = lw["conv_dw"].shape[1]
    full = lambda shape: pl.BlockSpec(shape, lambda i: (0,) * len(shape))
    args = (x_tm, mod, hist_tm, lw["norm1_g"], lw["w_in"], lw["q_norm_g"], lw["k_norm_g"],
            lw["headmean"], lw["conv_dw"], lw["conv_dw_b"], lw["conv_ln_g"], lw["conv_ln_b"],
            lw["w_pw2"])
    return pl.pallas_call(
        functools.partial(_sample_front_kernel, t_new=t_new),
        grid=(1,),
        in_specs=[full(a.shape) for a in args],
        out_specs=[full((rows, ATTN_W))] * 3 + [full((rows, cw)), full((rows, cw))],
        out_shape=[jax.ShapeDtypeStruct((rows, ATTN_W), F32)] * 3 + [
            jax.ShapeDtypeStruct((rows, cw), F32), jax.ShapeDtypeStruct((rows, cw), BF16)],
        scratch_shapes=[pltpu.VMEM((rows, d), BF16), pltpu.VMEM((rows, cw), BF16)],
        compiler_params=_params(("arbitrary",)),
        name="sample_front",
    )(*args)


def _window_attend(qbd, table, kn_ref, vn_ref, ck_ref, cv_ref, hmask_ref, nk_ref, nv_ref, o_ref,
                   kb, vb, *, t_new, n_patterns, row_chunk):
    width, n_hist = ck_ref.shape
    keep = n_hist - LANES
    lane = lax.broadcasted_iota(jnp.int32, (row_chunk, LANES), 1)
    is_new = lane >= LANES - t_new
    for src_ref, new_ref, dst_ref, buf in ((ck_ref, kn_ref, nk_ref, kb), (cv_ref, vn_ref, nv_ref, vb)):
        for c in range(0, width, row_chunk):
            rows = slice(c, c + row_chunk)
            old = src_ref[rows, :]
            new = jnp.concatenate([jnp.zeros((row_chunk, LANES - t_new), F32), new_ref[rows, :]], axis=1)
            buf[rows, 0:n_hist] = old.astype(BF16)
            buf[rows, n_hist:] = new.astype(BF16)
            shifted = pltpu.roll(old, n_hist - t_new, axis=1)
            dst_ref[rows, 0:keep] = shifted[:, 0:keep]
            dst_ref[rows, keep:] = jnp.where(is_new, new, shifted[:, keep:])

    s = _dot(qbd, kb[...])
    ms, ls, exs = [], [], []
    for p in range(n_patterns):
        sp = s + table(p)
        m = jnp.max(sp, axis=-1, keepdims=True)
        ex = jnp.exp(sp - m)
        ms.append(m)
        ls.append(jnp.sum(ex, axis=-1, keepdims=True))
        exs.append(ex.astype(BF16))
    pv = lax.dot_general(jnp.concatenate(exs, axis=0), vb[...], _NT, preferred_element_type=F32)
    nr = s.shape[0]
    accs = [pv[p * nr:(p + 1) * nr, :] for p in range(n_patterns)]
    top = functools.reduce(jnp.maximum, ms)
    ws = [jnp.exp(m - top) for m in ms]
    num = sum(w * a for w, a in zip(ws, accs))
    den = sum(w * l for w, l in zip(ws, ls))
    rows = num * (1.0 / den) * hmask_ref[...]
    heads = nr // t_new
    for t in range(t_new):
        o_ref[t:t + 1, :] = jnp.sum(rows[t * heads:(t + 1) * heads, :], axis=0, keepdims=True)


def _sample_back_kernel(x_ref, mod_ref, oa_ref, oc_ref, wo_ref, n2g_ref, w1_ref, w2_ref,
                        y_ref, a_ref, *, ff_chunk):
    aw = oa_ref.shape[1]
    mix = _dot(oa_ref[...].astype(BF16), wo_ref[0:aw, :]) + _dot(oc_ref[...], wo_ref[aw:, :])
    y_ref[...] = _mlp_tail(x_ref[...], mix, mod_ref[...], n2g_ref, w1_ref, w2_ref, a_ref, ff_chunk)


def _sample_back(x_tm, mod, o_attn_tm, o_conv_tm, lw, t_new, ff_chunk=512):
    rows, d = x_tm.shape
    nb = rows // t_new
    dff = lw["w_ff1"].shape[1]
    cw = o_conv_tm.shape[1]
    tok = lambda width: pl.BlockSpec((nb, width), lambda t: (t, 0))
    return pl.pallas_call(
        functools.partial(_sample_back_kernel, ff_chunk=ff_chunk),
        grid=(t_new,),
        in_specs=[tok(d), _resident(mod.shape), tok(ATTN_W), tok(cw),
                  _resident((d, d)), _resident((1, d)), _resident((d, dff)), _resident((dff, d))],
        out_specs=tok(d),
        out_shape=jax.ShapeDtypeStruct((rows, d), F32),
        scratch_shapes=[pltpu.VMEM((nb, dff), BF16)],
        compiler_params=_params(("arbitrary",)),
        name="sample_back",
    )(x_tm, mod, o_attn_tm, o_conv_tm, lw["w_o"], lw["norm2_g"], lw["w_ff1"], lw["w_ff2"])


def _layer_weights(l, norm1_g, norm2_g, w_in, q_norm_g, k_norm_g, conv_dw, conv_dw_b,
                   conv_ln_g, conv_ln_b, w_pw2, w_o, w_ff1, w_ff2):
    head_of = jnp.arange(ATTN_W) // HEAD_DIM
    headmean = (head_of[:, None] == head_of[None, :]).astype(F32) / HEAD_DIM
    expand = (jnp.arange(2 * LANES)[:, None] % LANES == head_of[None, :]).astype(BF16)
    row = lambda a: a[l].reshape(1, -1)
    return {
        "norm1_g": row(norm1_g), "norm2_g": row(norm2_g),
        "w_in": w_in[l].astype(BF16),
        "q_norm_g": jnp.tile(q_norm_g[l], N_HEADS).reshape(1, ATTN_W),
        "k_norm_g": jnp.tile(k_norm_g[l], N_HEADS).reshape(1, ATTN_W),
        "headmean": headmean.astype(BF16), "expand": expand,
        "conv_dw": conv_dw[l], "conv_dw_b": row(conv_dw_b),
        "conv_ln_g": row(conv_ln_g), "conv_ln_b": row(conv_ln_b),
        "w_pw2": w_pw2[l].astype(BF16), "w_o": w_o[l].astype(BF16),
        "w_ff1": w_ff1[l].astype(BF16), "w_ff2": w_ff2[l].astype(BF16),
    }


def _feature_major(cache):
    nb, n_hist = cache.shape[0], cache.shape[1]
    return cache.transpose(0, 2, 3, 1).reshape(nb, ATTN_W, n_hist)


def _position_major(cache_t):
    nb, _, n_hist = cache_t.shape
    return cache_t.reshape(nb, N_HEADS, HEAD_DIM, n_hist).transpose(0, 3, 1, 2)


def kernel(x_prompt, x_sample, cache_k, cache_v, state_conv, c_prompt, c_sample, rel_bias, norm1_g, norm2_g, w_ada, b_ada, w_in, q_norm_g, k_norm_g, conv_dw, conv_dw_b, conv_ln_g, conv_ln_b, w_pw2, w_o, w_ff1, w_ff2):
    depth = w_in.shape[0]
    b, s, d = x_prompt.shape
    nb, t_new, _ = x_sample.shape
    n_hist = cache_k.shape[2]
    assert s % WINDOW_MAX == 0 and n_hist % LANES == 0 and t_new <= LANES

    prompt_bias = _prompt_bias_tables(rel_bias)
    n_groups = SAMPLE_HEAD_GROUPS
    hpg = N_HEADS // n_groups
    gw = hpg * HEAD_DIM
    n_keys = n_hist + LANES
    sample_tabs = _sample_bias_tables(rel_bias, n_hist, t_new)
    sample_tabs = sample_tabs.reshape(len(PATTERNS), t_new, n_groups, hpg, n_keys).transpose(
        2, 0, 1, 3, 4).reshape(n_groups, len(PATTERNS), t_new * hpg, n_keys)
    hmask_g = (jnp.arange(hpg)[:, None] == (jnp.arange(gw) // HEAD_DIM)[None, :]).astype(F32)
    hmask = jnp.tile(hmask_g, (t_new, 1))

    xp = x_prompt
    xs_tm = x_sample.transpose(1, 0, 2).reshape(t_new * nb, d)
    c_all = jnp.concatenate([c_prompt, c_sample], axis=0)
    kp_l, vp_l, cp_l, ks_l, vs_l, cs_l = [], [], [], [], [], []
    for l in range(depth):
        lw = _layer_weights(l, norm1_g, norm2_g, w_in, q_norm_g, k_norm_g, conv_dw, conv_dw_b,
                            conv_ln_g, conv_ln_b, w_pw2, w_o, w_ff1, w_ff2)
        mod = _modulation(c_all, w_ada[l], b_ada[l])
        mod_p, mod_s = mod[:b], mod[b:]

        hist_tm = state_conv[l].transpose(1, 0, 2)
        q_s, k_s, v_s, glu_s, oc_s = _sample_front(xs_tm, mod_s, hist_tm, lw, t_new)
        to_bm = lambda a: a.reshape(t_new, nb, -1).transpose(1, 0, 2)
        q_bm = (to_bm(q_s) * (HEAD_DIM ** -0.5)).reshape(nb, t_new, n_groups, gw).transpose(0, 2, 1, 3)
        qbd = (q_bm[:, :, :, None, :] * hmask_g[None, None, None]).reshape(nb, n_groups, t_new * hpg, gw)
        new_t = lambda a: a.reshape(t_new, nb, ATTN_W).transpose(1, 2, 0)
        sample = (qbd.astype(BF16), new_t(k_s), new_t(v_s),
                  _feature_major(cache_k[l]), _feature_major(cache_v[l]), sample_tabs, hmask)

        qkv1, qkv4, qkv16, glu, k_tail, v_tail = _prompt_front(xp, mod_p, lw)
        outs, stats = [], []
        for pi, qkv in enumerate((qkv1.reshape(b, 1, s, -1), qkv4, qkv16)):
            o_d, st_d = _prompt_attention(qkv, prompt_bias[pi])
            outs.append(o_d)
            stats.append(st_d)
        xp, new_k_t, new_v_t, o_s = _prompt_back(xp, mod_p, outs, stats, glu, lw, sample)
        kp_l.append(k_tail.reshape(b, -1, N_HEADS, HEAD_DIM))
        vp_l.append(v_tail.reshape(b, -1, N_HEADS, HEAD_DIM))
        cp_l.append(glu[:, s - CONV_HIST:, :])

        o_s_tm = o_s.transpose(2, 0, 1, 3).reshape(t_new * nb, ATTN_W)
        xs_tm = _sample_back(xs_tm, mod_s, o_s_tm, oc_s, lw, t_new)
        ks_l.append(_position_major(new_k_t))
        vs_l.append(_position_major(new_v_t))
        cs_l.append(jnp.concatenate([state_conv[l].astype(F32), to_bm(glu_s)], axis=1)[:, -CONV_HIST:])

    xs = xs_tm.reshape(t_new, nb, d).transpose(1, 0, 2)
    return (xp, xs, jnp.stack(kp_l), jnp.stack(vp_l), jnp.stack(cp_l),
            jnp.stack(ks_l), jnp.stack(vs_l), jnp.stack(cs_l))
```

```python
import functools
import math

import jax
import jax.numpy as jnp
from jax import lax
from jax.experimental import pallas as pl
from jax.experimental.pallas import tpu as pltpu

F32 = jnp.float32
BF16 = jnp.bfloat16

HEAD_DIM = 64
N_HEADS = 8
ATTN_W = N_HEADS * HEAD_DIM
CONV_K = 31
CONV_HIST = CONV_K - 1
PATTERNS = ((128, 1), (512, 4), (2048, 16))
WINDOW_MAX = 2048
BAND = 128
NUM_BUCKETS = 32
EPS = 1e-6
N_MOD = 6
LANES = 128
SUBLANES = 8
HALO = 32
CONV_PITCH = 3
VMEM_LIMIT = 56 * 1024 * 1024
N_SLABS = ATTN_W // LANES
SAMPLE_HEAD_GROUPS = 2

_NT = (((1,), (1,)), ((), ()))


def _dot(a, b):
    return jnp.dot(a, b, preferred_element_type=F32)


def _resident(shape):
    return pl.BlockSpec(shape, lambda *_: (0,) * len(shape), pipeline_mode=pl.Buffered(1))


def _params(sem, flags=None):
    return pltpu.CompilerParams(dimension_semantics=sem, vmem_limit_bytes=VMEM_LIMIT, flags=flags)


def _mod_kernel(c_ref, w_ref, b_ref, o_ref):
    c = c_ref[...]
    a = c * jax.nn.sigmoid(c)
    w = w_ref[...]
    a_hi = a.astype(BF16)
    a_lo = (a - a_hi.astype(F32)).astype(BF16)
    w_hi = w.astype(BF16)
    w_lo = (w - w_hi.astype(F32)).astype(BF16)
    o_ref[...] = _dot(a_hi, w_hi) + _dot(a_lo, w_hi) + _dot(a_hi, w_lo) + b_ref[...]


def _modulation(c_all, w_ada, b_ada):
    rows, d = c_all.shape
    n = w_ada.shape[1]
    tn = d
    return pl.pallas_call(
        _mod_kernel,
        grid=(n // tn,),
        in_specs=[pl.BlockSpec((rows, d), lambda j: (0, 0)),
                  pl.BlockSpec((d, tn), lambda j: (0, j)),
                  pl.BlockSpec((1, tn), lambda j: (0, j))],
        out_specs=pl.BlockSpec((rows, tn), lambda j: (0, j)),
        out_shape=jax.ShapeDtypeStruct((rows, n), F32),
        compiler_params=_params(("arbitrary",)),
        name="mod",
    )(c_all, w_ada, b_ada.reshape(1, n))


def _rel_bucket(dist):
    max_exact = NUM_BUCKETS // 2
    df = jnp.maximum(dist, 1).astype(F32)
    large = max_exact + (jnp.log(df / max_exact) / math.log(WINDOW_MAX / max_exact)
                         * (NUM_BUCKETS - max_exact)).astype(jnp.int32)
    large = jnp.minimum(large, NUM_BUCKETS - 1)
    return jnp.where(dist < max_exact, dist, large)


def _prompt_bias_kernel(rel_ref, bkt_ref, o_ref):
    bkt = bkt_ref[0]
    for h in range(o_ref.shape[1]):
        acc = jnp.full(bkt.shape, -jnp.inf, F32)
        for b in range(NUM_BUCKETS):
            acc = jnp.where(bkt == b, rel_ref[b, h], acc)
        o_ref[0, h] = acc


def _prompt_bias_tables(rel_bias):
    i = jnp.arange(BAND)[:, None]
    jj = jnp.arange(2 * BAND)[None, :]
    delta = BAND + i - jj
    valid = (delta >= 0) & (delta <= BAND)
    bkt = jnp.stack([jnp.where(valid, _rel_bucket(jnp.maximum(delta, 0) * d), -1)
                     for (_, d) in PATTERNS]).astype(jnp.int32)
    npat = len(PATTERNS)
    return pl.pallas_call(
        _prompt_bias_kernel,
        grid=(npat,),
        in_specs=[pl.BlockSpec(memory_space=pltpu.SMEM),
                  pl.BlockSpec((1, BAND, 2 * BAND), lambda p: (p, 0, 0))],
        out_specs=pl.BlockSpec((1, N_HEADS, BAND, 2 * BAND), lambda p: (p, 0, 0, 0)),
        out_shape=jax.ShapeDtypeStruct((npat, N_HEADS, BAND, 2 * BAND), F32),
        compiler_params=_params(("arbitrary",)),
        name="prompt_bias",
    )(rel_bias, bkt)


def _sample_bias_kernel(relrow_ref, bkt_ref, o_ref):
    bkt = bkt_ref[0]
    reps = bkt.shape[1] // LANES
    acc = jnp.full(bkt.shape, -jnp.inf, F32)
    for b in range(NUM_BUCKETS):
        acc = jnp.where(bkt == b, jnp.tile(relrow_ref[b], (1, reps)), acc)
    o_ref[0] = acc


def _sample_bias_tables(rel_bias, n_hist, t_new):
    n_keys = n_hist + LANES
    c = jnp.arange(n_keys)[None, :]
    t = jnp.arange(t_new)[:, None]
    key = jnp.where(c < n_hist, c, c - (LANES - t_new))
    real = (c < n_hist) | (c >= n_keys - t_new)
    dist = n_hist + t - key
    tabs = []
    for (w, d) in PATTERNS:
        valid = real & (dist >= 0) & (dist <= w) & (dist % d == 0)
        tabs.append(jnp.where(valid, _rel_bucket(jnp.maximum(dist, 0)), -1))
    bkt = jnp.stack(tabs).astype(jnp.int32)
    bkt = jnp.repeat(bkt, N_HEADS, axis=1)
    relrow = jnp.broadcast_to(jnp.tile(rel_bias, (1, t_new))[:, :, None],
                              (NUM_BUCKETS, t_new * N_HEADS, LANES))
    npat = len(PATTERNS)
    rows = t_new * N_HEADS
    return pl.pallas_call(
        _sample_bias_kernel,
        grid=(npat,),
        in_specs=[pl.BlockSpec((NUM_BUCKETS, rows, LANES), lambda p: (0, 0, 0)),
                  pl.BlockSpec((1, rows, n_keys), lambda p: (p, 0, 0))],
        out_specs=pl.BlockSpec((1, rows, n_keys), lambda p: (p, 0, 0)),
        out_shape=jax.ShapeDtypeStruct((npat, rows, n_keys), F32),
        compiler_params=_params(("arbitrary",)),
        name="sample_bias",
    )(relrow, bkt)


def _modulated_rms(x, g, scale, shift):
    y = x * lax.rsqrt(jnp.mean(x * x, axis=-1, keepdims=True) + EPS) * g
    return y * (1.0 + scale) + shift


def _mod_slice(mod, k, d):
    return mod[:, k * d:(k + 1) * d]


def _head_norm(t, g, headmean_ref):
    ms = _dot((t * t).astype(BF16), headmean_ref[...])
    return t * lax.rsqrt(ms + EPS) * g


def _in_proj(hb, w_in_ref, qg_ref, kg_ref, headmean_ref):
    aw = ATTN_W
    q = _head_norm(_dot(hb, w_in_ref[:, 0:aw]), qg_ref[...], headmean_ref)
    k = _head_norm(_dot(hb, w_in_ref[:, aw:2 * aw]), kg_ref[...], headmean_ref)
    v = _dot(hb, w_in_ref[:, 2 * aw:3 * aw])
    cw = (w_in_ref.shape[1] - 3 * aw) // 2
    ua = _dot(hb, w_in_ref[:, 3 * aw:3 * aw + cw])
    ub = _dot(hb, w_in_ref[:, 3 * aw + cw:3 * aw + 2 * cw])
    return q, k, v, ua * jax.nn.sigmoid(ub)


def _conv_post(dw, lng_ref, lnb_ref):
    mu = jnp.mean(dw, axis=-1, keepdims=True)
    xc = dw - mu
    var = jnp.mean(xc * xc, axis=-1, keepdims=True)
    y = xc * lax.rsqrt(var + EPS) * lng_ref[...] + lnb_ref[...]
    return (y * jax.nn.sigmoid(y)).astype(BF16)


def _mlp_tail(x, mix, mod, n2g_ref, w1_ref, w2_ref, a_ref, ff_chunk):
    d = x.shape[-1]
    x1 = x + _mod_slice(mod, 2, d) * mix
    h2 = _modulated_rms(x1, n2g_ref[...], _mod_slice(mod, 4, d), _mod_slice(mod, 3, d)).astype(BF16)
    for c in range(w1_ref.shape[1] // ff_chunk):
        a = jnp.maximum(_dot(h2, w1_ref[:, c * ff_chunk:(c + 1) * ff_chunk]), 0.0)
        a_ref[:, c * ff_chunk:(c + 1) * ff_chunk] = (a * a).astype(BF16)
    return x1 + _mod_slice(mod, 5, d) * _dot(a_ref[...], w2_ref[...])


def _slabs(x):
    return [x[:, s * LANES:(s + 1) * LANES] for s in range(x.shape[1] // LANES)]


def _regroup_store(x, nat, p4, out1_ref, out4_ref, out16_ref, which):
    tm = x.shape[0]
    d4 = PATTERNS[1][1]
    d16 = PATTERNS[2][1]
    sub = d16 // d4
    cols = slice(which * ATTN_W, (which + 1) * ATTN_W)
    out1_ref[0, :, cols] = x.astype(BF16)
    for s, xs in enumerate(_slabs(x)):
        nat[s] = xs
    for r4 in range(d4):
        parts = []
        for s in range(N_SLABS):
            part = nat[s, pl.ds(r4, tm // d4, stride=d4), :]
            p4[s, r4] = part
            parts.append(part)
        out4_ref[0, r4, :, cols] = jnp.concatenate(parts, axis=1).astype(BF16)
    for r16 in range(d16):
        r4, a = r16 % d4, r16 // d4
        parts = [p4[s, r4, pl.ds(a, tm // d16, stride=sub), :] for s in range(N_SLABS)]
        out16_ref[0, r16, :, cols] = jnp.concatenate(parts, axis=1).astype(BF16)


def _front_kernel(x_ref, mod_ref, n1g_ref, w_in_ref, qg_ref, kg_ref, headmean_ref,
                  qkv1_ref, qkv4_ref, qkv16_ref, glu_ref, kf_ref, vf_ref, nat, p4):
    d = x_ref.shape[2]
    mod = mod_ref[0]
    h = _modulated_rms(x_ref[0], n1g_ref[...], _mod_slice(mod, 1, d), _mod_slice(mod, 0, d))
    q, k, v, glu = _in_proj(h.astype(BF16), w_in_ref, qg_ref, kg_ref, headmean_ref)
    glu_ref[0] = glu
    kf_ref[0] = k
    vf_ref[0] = v
    for which, t in enumerate((q * (HEAD_DIM ** -0.5), k, v)):
        _regroup_store(t, nat, p4, qkv1_ref, qkv4_ref, qkv16_ref, which)


def _prompt_front(x, mod, lw, tm=1024):
    b, s, d = x.shape
    win = min(WINDOW_MAX, s)
    tail_start = (s - win) // tm
    cw = (lw["w_in"].shape[1] - 3 * ATTN_W) // 2
    d4, d16 = PATTERNS[1][1], PATTERNS[2][1]
    tok = lambda width: pl.BlockSpec((1, tm, width), lambda bi, i: (bi, i, 0))
    qkv_w = 3 * ATTN_W
    plane = lambda dil: pl.BlockSpec((1, dil, tm // dil, qkv_w), lambda bi, i: (bi, 0, i, 0))
    tail = pl.BlockSpec((1, tm, ATTN_W), lambda bi, i: (bi, jnp.maximum(i - tail_start, 0), 0))
    in_specs = [tok(d),
                pl.BlockSpec((1, 1, N_MOD * d), lambda bi, i: (bi, 0, 0)),
                _resident((1, d)), _resident(lw["w_in"].shape), _resident((1, ATTN_W)),
                _resident((1, ATTN_W)), _resident((ATTN_W, ATTN_W))]
    out_specs = [tok(qkv_w), plane(d4), plane(d16), tok(cw), tail, tail]
    plane_shape = lambda dil: jax.ShapeDtypeStruct((b, dil, s // dil, qkv_w), BF16)
    out_shape = ([jax.ShapeDtypeStruct((b, s, qkv_w), BF16), plane_shape(d4), plane_shape(d16)] + [
        jax.ShapeDtypeStruct((b, s, cw), F32),
        jax.ShapeDtypeStruct((b, win, ATTN_W), F32), jax.ShapeDtypeStruct((b, win, ATTN_W), F32)])
    return pl.pallas_call(
        _front_kernel,
        grid=(b, s // tm),
        in_specs=in_specs, out_specs=out_specs, out_shape=out_shape,
        scratch_shapes=[pltpu.VMEM((N_SLABS, tm, LANES), F32),
                        pltpu.VMEM((N_SLABS, d4, tm // d4, LANES), F32)],
        compiler_params=_params(("arbitrary", "arbitrary")),
        name="prompt_front",
    )(x, mod.reshape(b, 1, N_MOD * d), lw["norm1_g"], lw["w_in"], lw["q_norm_g"], lw["k_norm_g"],
      lw["headmean"])


def _attn_kernel(q_ref, kp_ref, kc_ref, vp_ref, vc_ref, bias_ref, o_ref, st_ref):
    i = pl.program_id(2)
    qb = q_ref.shape[2]
    lane = lax.broadcasted_iota(jnp.int32, (BAND, LANES), 1)
    low = lane < HEAD_DIM
    col = lax.broadcasted_iota(jnp.int32, (1, 2 * BAND), 1)
    first_mask = jnp.where((col < BAND) & (i == 0), -jnp.inf, 0.0)
    zero = jnp.zeros((), BF16)
    ones = jnp.ones((2 * BAND, LANES), BF16)

    def window(prev_ref, cur_ref, j, cols):
        if j == 0:
            return jnp.concatenate([prev_ref[0, 0, :, cols], cur_ref[0, 0, 0:BAND, cols]], axis=0)
        return cur_ref[0, 0, (j - 1) * BAND:(j + 1) * BAND, cols]

    for j in range(qb // BAND):
        rows = slice(j * BAND, (j + 1) * BAND)
        stat = jnp.zeros((BAND, LANES), F32)
        for p in range(N_SLABS):
            cols = slice(p * LANES, (p + 1) * LANES)
            q2 = q_ref[0, 0, rows, cols]
            k2 = window(kp_ref, kc_ref, j, cols)
            v2 = jnp.concatenate([window(vp_ref, vc_ref, j, cols), ones], axis=1)
            halves = []
            for e in range(LANES // HEAD_DIM):
                hd = p * (LANES // HEAD_DIM) + e
                qm = jnp.where(low if e == 0 else jnp.logical_not(low), q2, zero)
                s = lax.dot_general(qm, k2, _NT, preferred_element_type=F32) + bias_ref[hd]
                if j == 0:
                    s = s + first_mask
                m = jnp.max(s, axis=-1, keepdims=True)
                pv = _dot(jnp.exp(s - m).astype(BF16), v2)
                l = pv[:, LANES:]
                halves.append(pv[:, :LANES] * (1.0 / l))
                stat = jnp.where(lane == hd, m + jnp.log(l), stat)
            o_ref[0, 0, rows, cols] = jnp.where(low, halves[0], halves[1]).astype(BF16)
        st_ref[0, 0, rows, :] = stat


def _prompt_attention(qkv, bias, qb=512):
    b, dil, u, _ = qkv.shape
    qb = min(qb, u)
    ratio = qb // BAND
    cur = lambda col: pl.BlockSpec((1, 1, qb, ATTN_W), lambda bi, r, i: (bi, r, i, col))
    prev = lambda col: pl.BlockSpec(
        (1, 1, BAND, ATTN_W), lambda bi, r, i: (bi, r, jnp.maximum(i * ratio - 1, 0), col))
    return pl.pallas_call(
        _attn_kernel,
        grid=(b, dil, u // qb),
        in_specs=[cur(0), prev(1), cur(1), prev(2), cur(2), _resident(bias.shape)],
        out_specs=[cur(0), pl.BlockSpec((1, 1, qb, LANES), lambda bi, r, i: (bi, r, i, 0))],
        out_shape=[jax.ShapeDtypeStruct((b, dil, u, ATTN_W), BF16),
                   jax.ShapeDtypeStruct((b, dil, u, LANES), F32)],
        compiler_params=_params(("arbitrary", "arbitrary", "arbitrary")),
        name=f"prompt_attn_d{dil}",
    )(qkv, qkv, qkv, qkv, qkv, bias)


def _back_kernel(x_ref, mod_ref, o1_ref, o4_ref, o16_ref, s1_ref, s4_ref, s16_ref,
                 glu_ref, halo_ref, cw_ref, cb_ref, lng_ref, lnb_ref, wpw_ref,
                 expand_ref, wo_ref, n2g_ref, w1_ref, w2_ref,
                 qbd_ref, kn_ref, vn_ref, ck_ref, cv_ref, tab_ref, hmask_ref,
                 y_ref, nk_ref, nv_ref, os_ref,
                 lnat, onat, gbuf, act, a_ref, kb, vb, *, row_chunk, ff_chunk, t_new, n_groups):
    tm = x_ref.shape[1]
    group = (pl.program_id(0) * pl.num_programs(1) + pl.program_id(1)) % n_groups
    _window_attend(qbd_ref[0, 0], lambda p: tab_ref[group, p],
                   kn_ref.at[0], vn_ref.at[0], ck_ref.at[0], cv_ref.at[0], hmask_ref,
                   nk_ref.at[0], nv_ref.at[0], os_ref.at[0, 0], kb, vb,
                   t_new=t_new, n_patterns=tab_ref.shape[1], row_chunk=row_chunk)

    def steps(first, count):
        return pl.ds(CONV_PITCH * first, count, stride=CONV_PITCH)

    n_cs = gbuf.shape[0]
    has_prev = (pl.program_id(1) > 0).astype(F32)
    for s, part in enumerate(_slabs(halo_ref[0] * has_prev)):
        gbuf[s, steps(0, HALO), :] = part
    for s, part in enumerate(_slabs(glu_ref[0])):
        gbuf[s, steps(HALO, tm), :] = part
    for which, (o_ref, s_ref) in enumerate(((o4_ref, s4_ref), (o16_ref, s16_ref))):
        dil = o_ref.shape[1]
        for r in range(dil):
            rows = pl.ds(r, tm // dil, stride=dil)
            lnat[which, rows, :] = s_ref[0, r]
            for s, part in enumerate(_slabs(o_ref[0, r])):
                onat[which, s, rows, :] = part.astype(F32)

    lead = HALO - CONV_HIST
    for c in range(tm // row_chunk):
        accs = [jnp.broadcast_to(b_s, (row_chunk, LANES)) for b_s in _slabs(cb_ref[...])]
        for j in range(CONV_K):
            for s in range(n_cs):
                accs[s] = accs[s] + (gbuf[s, steps(c * row_chunk + lead + j, row_chunk), :]
                                     * cw_ref[j:j + 1, s * LANES:(s + 1) * LANES])
        act[c * row_chunk:(c + 1) * row_chunk, :] = _conv_post(
            jnp.concatenate(accs, axis=1), lng_ref, lnb_ref)
    o_conv = _dot(act[...], wpw_ref[...]).astype(BF16)

    ls = [s1_ref[0, 0], lnat[0], lnat[1]]
    top = jnp.maximum(jnp.maximum(ls[0], ls[1]), ls[2])
    ps = [jnp.exp(t - top) for t in ls]
    inv = 1.0 / (ps[0] + ps[1] + ps[2])
    os = [o1_ref[0, 0].astype(F32)] + [
        jnp.concatenate([onat[which, s] for s in range(N_SLABS)], axis=1) for which in range(2)]
    o_attn = None
    for p_d, o_d in zip(ps, os):
        w = p_d * inv
        w_hi = w.astype(BF16)
        w_lo = (w - w_hi.astype(F32)).astype(BF16)
        w_full = _dot(jnp.concatenate([w_hi, w_lo], axis=1), expand_ref[...])
        term = w_full * o_d
        o_attn = term if o_attn is None else o_attn + term
    aw = o_attn.shape[1]
    mix = _dot(o_attn.astype(BF16), wo_ref[0:aw, :]) + _dot(o_conv, wo_ref[aw:, :])
    y_ref[0] = _mlp_tail(x_ref[0], mix, mod_ref[0], n2g_ref, w1_ref, w2_ref, a_ref, ff_chunk)


def _prompt_back(x, mod, outs, stats, glu, lw, sample, row_chunk=64, ff_chunk=1024):
    b, s, d = x.shape
    dff = lw["w_ff1"].shape[1]
    cw = glu.shape[2]
    qbd, k_new_t, v_new_t, cache_k_t, cache_v_t, tabs, hmask = sample
    nb, n_groups, rows_g, gw = qbd.shape
    n_hist = cache_k_t.shape[2]
    t_new = rows_g // (gw // HEAD_DIM)
    tm = (b * s) // (nb * n_groups)
    assert b * s == tm * nb * n_groups and s % tm == 0 and tm % row_chunk == 0
    n = s // tm
    unit = lambda bi, i: ((bi * n + i) // n_groups, (bi * n + i) % n_groups)
    hosted = lambda shape: pl.BlockSpec((1,) + shape, lambda bi, i: unit(bi, i) + (0,))
    hosted4 = lambda shape: pl.BlockSpec((1, 1) + shape, lambda bi, i: unit(bi, i) + (0, 0))
    tok = lambda width: pl.BlockSpec((1, tm, width), lambda bi, i: (bi, i, 0))
    plane = lambda a: pl.BlockSpec((1, a.shape[1], tm // a.shape[1], a.shape[3]),
                                   lambda bi, i: (bi, 0, i, 0))
    halo = pl.BlockSpec((1, HALO, cw), lambda bi, i: (bi, jnp.maximum(i * (tm // HALO) - 1, 0), 0))
    in_specs = ([tok(d), pl.BlockSpec((1, 1, N_MOD * d), lambda bi, i: (bi, 0, 0))]
                + [plane(a) for a in outs] + [plane(a) for a in stats]
                + [tok(cw), halo,
                   _resident((CONV_K, cw)), _resident((1, cw)), _resident((1, cw)), _resident((1, cw)),
                   _resident((cw, cw)),
                   _resident(lw["expand"].shape), _resident((d, d)), _resident((1, d)),
                   _resident((d, dff)), _resident((dff, d)),
                   hosted4((rows_g, gw)), hosted((gw, t_new)), hosted((gw, t_new)),
                   hosted((gw, n_hist)), hosted((gw, n_hist)),
                   _resident(tabs.shape), _resident(hmask.shape)])
    out_specs = [tok(d), hosted((gw, n_hist)), hosted((gw, n_hist)), hosted4((t_new, gw))]
    out_shape = [jax.ShapeDtypeStruct((b, s, d), F32),
                 jax.ShapeDtypeStruct(cache_k_t.shape, cache_k_t.dtype),
                 jax.ShapeDtypeStruct(cache_v_t.shape, cache_v_t.dtype),
                 jax.ShapeDtypeStruct((nb, n_groups, t_new, gw), F32)]
    return pl.pallas_call(
        functools.partial(_back_kernel, row_chunk=row_chunk, ff_chunk=ff_chunk,
                          t_new=t_new, n_groups=n_groups),
        grid=(b, n),
        in_specs=in_specs, out_specs=out_specs, out_shape=out_shape,
        scratch_shapes=[pltpu.VMEM((2, tm, LANES), F32),
                        pltpu.VMEM((2, N_SLABS, tm, LANES), F32),
                        pltpu.VMEM((cw // LANES, CONV_PITCH * (tm + HALO), LANES), F32),
                        pltpu.VMEM((tm, cw), BF16), pltpu.VMEM((tm, dff), BF16),
                        pltpu.VMEM((gw, n_hist + LANES), BF16), pltpu.VMEM((gw, n_hist + LANES), BF16)],
        compiler_params=_params(("arbitrary", "arbitrary")),
        name="prompt_back",
    )(x, mod.reshape(b, 1, N_MOD * d), *outs, *stats, glu, glu, lw["conv_dw"], lw["conv_dw_b"], lw["conv_ln_g"],
      lw["conv_ln_b"], lw["w_pw2"], lw["expand"], lw["w_o"], lw["norm2_g"], lw["w_ff1"], lw["w_ff2"],
      qbd, k_new_t, v_new_t, cache_k_t, cache_v_t, tabs, hmask)


def _sample_front_kernel(x_ref, mod_ref, hist_ref, n1g_ref, w_in_ref, qg_ref, kg_ref, headmean_ref,
                         cw_ref, cb_ref, lng_ref, lnb_ref, wpw_ref,
                         q_ref, k_ref, v_ref, glu_ref, oc_ref, hbuf, act, *, t_new):
    nb = mod_ref.shape[0]
    d = x_ref.shape[1]
    mod = mod_ref[...]
    for t in range(t_new):
        rows = slice(t * nb, (t + 1) * nb)
        h = _modulated_rms(x_ref[rows, :], n1g_ref[...], _mod_slice(mod, 1, d), _mod_slice(mod, 0, d))
        hbuf[rows, :] = h.astype(BF16)
    q, k, v, glu = _in_proj(hbuf[...], w_in_ref, qg_ref, kg_ref, headmean_ref)
    q_ref[...] = q
    k_ref[...] = k
    v_ref[...] = v
    glu_ref[...] = glu

    def slab(sidx):
        if sidx < CONV_HIST:
            return hist_ref[sidx]
        return glu_ref[(sidx - CONV_HIST) * nb:(sidx - CONV_HIST + 1) * nb, :]

    for t in range(t_new):
        acc = jnp.broadcast_to(cb_ref[...], (nb, glu.shape[1]))
        for j in range(CONV_K):
            acc = acc + slab(t + j) * cw_ref[j:j + 1, :]
        act[t * nb:(t + 1) * nb, :] = _conv_post(acc, lng_ref, lnb_ref)
    oc_ref[...] = _dot(act[...], wpw_ref[...]).astype(BF16)


def _sample_front(x_tm, mod, hist_tm, lw, t_new):
    rows, d = x_tm.shape
    cw = lw["conv_dw"].shape[1]
    full = lambda shape: pl.BlockSpec(shape, lambda i: (0,) * len(shape))
    args = (x_tm, mod, hist_tm, lw["norm1_g"], lw["w_in"], lw["q_norm_g"], lw["k_norm_g"],
            lw["headmean"], lw["conv_dw"], lw["conv_dw_b"], lw["conv_ln_g"], lw["conv_ln_b"],
            lw["w_pw2"])
    return pl.pallas_call(
        functools.partial(_sample_front_kernel, t_new=t_new),
        grid=(1,),
        in_specs=[full(a.shape) for a in args],
        out_specs=[full((rows, ATTN_W))] * 3 + [full((rows, cw)), full((rows, cw))],
        out_shape=[jax.ShapeDtypeStruct((rows, ATTN_W), F32)] * 3 + [
            jax.ShapeDtypeStruct((rows, cw), F32), jax.ShapeDtypeStruct((rows, cw), BF16)],
        scratch_shapes=[pltpu.VMEM((rows, d), BF16), pltpu.VMEM((rows, cw), BF16)],
        compiler_params=_params(("arbitrary",)),
        name="sample_front",
    )(*args)


def _window_attend(qbd, table, kn_ref, vn_ref, ck_ref, cv_ref, hmask_ref, nk_ref, nv_ref, o_ref,
                   kb, vb, *, t_new, n_patterns, row_chunk):
    width, n_hist = ck_ref.shape
    keep = n_hist - LANES
    lane = lax.broadcasted_iota(jnp.int32, (row_chunk, LANES), 1)
    is_new = lane >= LANES - t_new
    for src_ref, new_ref, dst_ref, buf in ((ck_ref, kn_ref, nk_ref, kb), (cv_ref, vn_ref, nv_ref, vb)):
        for c in range(0, width, row_chunk):
            rows = slice(c, c + row_chunk)
            old = src_ref[rows, :]
            new = jnp.concatenate([jnp.zeros((row_chunk, LANES - t_new), F32), new_ref[rows, :]], axis=1)
            buf[rows, 0:n_hist] = old.astype(BF16)
            buf[rows, n_hist:] = new.astype(BF16)
            shifted = pltpu.roll(old, n_hist - t_new, axis=1)
            dst_ref[rows, 0:keep] = shifted[:, 0:keep]
            dst_ref[rows, keep:] = jnp.where(is_new, new, shifted[:, keep:])

    s = _dot(qbd, kb[...])
    ms, ls, exs = [], [], []
    for p in range(n_patterns):
        sp = s + table(p)
        m = jnp.max(sp, axis=-1, keepdims=True)
        ex = jnp.exp(sp - m)
        ms.append(m)
        ls.append(jnp.sum(ex, axis=-1, keepdims=True))
        exs.append(ex.astype(BF16))
    pv = lax.dot_general(jnp.concatenate(exs, axis=0), vb[...], _NT, preferred_element_type=F32)
    nr = s.shape[0]
    accs = [pv[p * nr:(p + 1) * nr, :] for p in range(n_patterns)]
    top = functools.reduce(jnp.maximum, ms)
    ws = [jnp.exp(m - top) for m in ms]
    num = sum(w * a for w, a in zip(ws, accs))
    den = sum(w * l for w, l in zip(ws, ls))
    rows = num * (1.0 / den) * hmask_ref[...]
    heads = nr // t_new
    for t in range(t_new):
        o_ref[t:t + 1, :] = jnp.sum(rows[t * heads:(t + 1) * heads, :], axis=0, keepdims=True)


def _sample_back_kernel(x_ref, mod_ref, oa_ref, oc_ref, wo_ref, n2g_ref, w1_ref, w2_ref,
                        y_ref, a_ref, *, ff_chunk):
    aw = oa_ref.shape[1]
    mix = _dot(oa_ref[...].astype(BF16), wo_ref[0:aw, :]) + _dot(oc_ref[...], wo_ref[aw:, :])
    y_ref[...] = _mlp_tail(x_ref[...], mix, mod_ref[...], n2g_ref, w1_ref, w2_ref, a_ref, ff_chunk)


def _sample_back(x_tm, mod, o_attn_tm, o_conv_tm, lw, t_new, ff_chunk=512):
    rows, d = x_tm.shape
    nb = rows // t_new
    dff = lw["w_ff1"].shape[1]
    cw = o_conv_tm.shape[1]
    tok = lambda width: pl.BlockSpec((nb, width), lambda t: (t, 0))
    return pl.pallas_call(
        functools.partial(_sample_back_kernel, ff_chunk=ff_chunk),
        grid=(t_new,),
        in_specs=[tok(d), _resident(mod.shape), tok(ATTN_W), tok(cw),
                  _resident((d, d)), _resident((1, d)), _resident((d, dff)), _resident((dff, d))],
        out_specs=tok(d),
        out_shape=jax.ShapeDtypeStruct((rows, d), F32),
        scratch_shapes=[pltpu.VMEM((nb, dff), BF16)],
        compiler_params=_params(("arbitrary",)),
        name="sample_back",
    )(x_tm, mod, o_attn_tm, o_conv_tm, lw["w_o"], lw["norm2_g"], lw["w_ff1"], lw["w_ff2"])


def _layer_weights(l, norm1_g, norm2_g, w_in, q_norm_g, k_norm_g, conv_dw, conv_dw_b,
                   conv_ln_g, conv_ln_b, w_pw2, w_o, w_ff1, w_ff2):
    head_of = jnp.arange(ATTN_W) // HEAD_DIM
    headmean = (head_of[:, None] == head_of[None, :]).astype(F32) / HEAD_DIM
    expand = (jnp.arange(2 * LANES)[:, None] % LANES == head_of[None, :]).astype(BF16)
    row = lambda a: a[l].reshape(1, -1)
    return {
        "norm1_g": row(norm1_g), "norm2_g": row(norm2_g),
        "w_in": w_in[l].astype(BF16),
        "q_norm_g": jnp.tile(q_norm_g[l], N_HEADS).reshape(1, ATTN_W),
        "k_norm_g": jnp.tile(k_norm_g[l], N_HEADS).reshape(1, ATTN_W),
        "headmean": headmean.astype(BF16), "expand": expand,
        "conv_dw": conv_dw[l], "conv_dw_b": row(conv_dw_b),
        "conv_ln_g": row(conv_ln_g), "conv_ln_b": row(conv_ln_b),
        "w_pw2": w_pw2[l].astype(BF16), "w_o": w_o[l].astype(BF16),
        "w_ff1": w_ff1[l].astype(BF16), "w_ff2": w_ff2[l].astype(BF16),
    }


def _feature_major(cache):
    nb, n_hist = cache.shape[0], cache.shape[1]
    return cache.transpose(0, 2, 3, 1).reshape(nb, ATTN_W, n_hist)


def _position_major(cache_t):
    nb, _, n_hist = cache_t.shape
    return cache_t.reshape(nb, N_HEADS, HEAD_DIM, n_hist).transpose(0, 3, 1, 2)


def kernel(x_prompt, x_sample, cache_k, cache_v, state_conv, c_prompt, c_sample, rel_bias, norm1_g, norm2_g, w_ada, b_ada, w_in, q_norm_g, k_norm_g, conv_dw, conv_dw_b, conv_ln_g, conv_ln_b, w_pw2, w_o, w_ff1, w_ff2):
    depth = w_in.shape[0]
    b, s, d = x_prompt.shape
    nb, t_new, _ = x_sample.shape
    n_hist = cache_k.shape[2]
    assert s % WINDOW_MAX == 0 and n_hist % LANES == 0 and t_new <= LANES

    prompt_bias = _prompt_bias_tables(rel_bias)
    n_groups = SAMPLE_HEAD_GROUPS
    hpg = N_HEADS // n_groups
    gw = hpg * HEAD_DIM
    n_keys = n_hist + LANES
    sample_tabs = _sample_bias_tables(rel_bias, n_hist, t_new)
    sample_tabs = sample_tabs.reshape(len(PATTERNS), t_new, n_groups, hpg, n_keys).transpose(
        2, 0, 1, 3, 4).reshape(n_groups, len(PATTERNS), t_new * hpg, n_keys)
    hmask_g = (jnp.arange(hpg)[:, None] == (jnp.arange(gw) // HEAD_DIM)[None, :]).astype(F32)
    hmask = jnp.tile(hmask_g, (t_new, 1))

    xp = x_prompt
    xs_tm = x_sample.transpose(1, 0, 2).reshape(t_new * nb, d)
    c_all = jnp.concatenate([c_prompt, c_sample], axis=0)
    kp_l, vp_l, cp_l, ks_l, vs_l, cs_l = [], [], [], [], [], []
    for l in range(depth):
        lw = _layer_weights(l, norm1_g, norm2_g, w_in, q_norm_g, k_norm_g, conv_dw, conv_dw_b,
                            conv_ln_g, conv_ln_b, w_pw2, w_o, w_ff1, w_ff2)
        mod = _modulation(c_all, w_ada[l], b_ada[l])
        mod_p, mod_s = mod[:b], mod[b:]

        hist_tm = state_conv[l].transpose(1, 0, 2)
        q_s, k_s, v_s, glu_s, oc_s = _sample_front(xs_tm, mod_s, hist_tm, lw, t_new)
        to_bm = lambda a: a.reshape(t_new, nb, -1).transpose(1, 0, 2)
        q_bm = (to_bm(q_s) * (HEAD_DIM ** -0.5)).reshape(nb, t_new, n_groups, gw).transpose(0, 2, 1, 3)
        qbd = (q_bm[:, :, :, None, :] * hmask_g[None, None, None]).reshape(nb, n_groups, t_new * hpg, gw)
        new_t = lambda a: a.reshape(t_new, nb, ATTN_W).transpose(1, 2, 0)
        sample = (qbd.astype(BF16), new_t(k_s), new_t(v_s),
                  _feature_major(cache_k[l]), _feature_major(cache_v[l]), sample_tabs, hmask)

        qkv1, qkv4, qkv16, glu, k_tail, v_tail = _prompt_front(xp, mod_p, lw)
        outs, stats = [], []
        for pi, qkv in enumerate((qkv1.reshape(b, 1, s, -1), qkv4, qkv16)):
            o_d, st_d = _prompt_attention(qkv, prompt_bias[pi])
            outs.append(o_d)
            stats.append(st_d)
        xp, new_k_t, new_v_t, o_s = _prompt_back(xp, mod_p, outs, stats, glu, lw, sample)
        kp_l.append(k_tail.reshape(b, -1, N_HEADS, HEAD_DIM))
        vp_l.append(v_tail.reshape(b, -1, N_HEADS, HEAD_DIM))
        cp_l.append(glu[:, s - CONV_HIST:, :])

        o_s_tm = o_s.transpose(2, 0, 1, 3).reshape(t_new * nb, ATTN_W)
        xs_tm = _sample_back(xs_tm, mod_s, o_s_tm, oc_s, lw, t_new)
        ks_l.append(_position_major(new_k_t))
        vs_l.append(_position_major(new_v_t))
        cs_l.append(jnp.concatenate([state_conv[l].astype(F32), to_bm(glu_s)], axis=1)[:, -CONV_HIST:])

    xs = xs_tm.reshape(t_new, nb, d).transpose(1, 0, 2)
    return (xp, xs, jnp.stack(kp_l), jnp.stack(vp_l), jnp.stack(cp_l),
            jnp.stack(ks_l), jnp.stack(vs_l), jnp.stack(cs_l))
```

```python
import functools
import math

import jax
import jax.numpy as jnp
from jax import lax
from jax.experimental import pallas as pl
from jax.experimental.pallas import tpu as pltpu

F32 = jnp.float32
BF16 = jnp.bfloat16

HEAD_DIM = 64
N_HEADS = 8
ATTN_W = N_HEADS * HEAD_DIM
CONV_K = 31
CONV_HIST = CONV_K - 1
PATTERNS = ((128, 1), (512, 4), (2048, 16))
WINDOW_MAX = 2048
BAND = 128
NUM_BUCKETS = 32
EPS = 1e-6
N_MOD = 6
LANES = 128
SUBLANES = 8
HALO = 32
CONV_PITCH = 3
VMEM_LIMIT = 56 * 1024 * 1024
N_SLABS = ATTN_W // LANES
SAMPLE_HEAD_GROUPS = 2

_NT = (((1,), (1,)), ((), ()))


def _dot(a, b):
    return jnp.dot(a, b, preferred_element_type=F32)


def _resident(shape):
    return pl.BlockSpec(shape, lambda *_: (0,) * len(shape), pipeline_mode=pl.Buffered(1))


def _params(sem, flags=None):
    return pltpu.CompilerParams(dimension_semantics=sem, vmem_limit_bytes=VMEM_LIMIT, flags=flags)


def _mod_kernel(c_ref, w_ref, b_ref, o_ref):
    c = c_ref[...]
    a = c * jax.nn.sigmoid(c)
    w = w_ref[...]
    a_hi = a.astype(BF16)
    a_lo = (a - a_hi.astype(F32)).astype(BF16)
    w_hi = w.astype(BF16)
    w_lo = (w - w_hi.astype(F32)).astype(BF16)
    o_ref[...] = _dot(a_hi, w_hi) + _dot(a_lo, w_hi) + _dot(a_hi, w_lo) + b_ref[...]


def _modulation(c_all, w_ada, b_ada):
    rows, d = c_all.shape
    n = w_ada.shape[1]
    tn = d
    return pl.pallas_call(
        _mod_kernel,
        grid=(n // tn,),
        in_specs=[pl.BlockSpec((rows, d), lambda j: (0, 0)),
                  pl.BlockSpec((d, tn), lambda j: (0, j)),
                  pl.BlockSpec((1, tn), lambda j: (0, j))],
        out_specs=pl.BlockSpec((rows, tn), lambda j: (0, j)),
        out_shape=jax.ShapeDtypeStruct((rows, n), F32),
        compiler_params=_params(("arbitrary",)),
        name="mod",
    )(c_all, w_ada, b_ada.reshape(1, n))


def _rel_bucket(dist):
    max_exact = NUM_BUCKETS // 2
    df = jnp.maximum(dist, 1).astype(F32)
    large = max_exact + (jnp.log(df / max_exact) / math.log(WINDOW_MAX / max_exact)
                         * (NUM_BUCKETS - max_exact)).astype(jnp.int32)
    large = jnp.minimum(large, NUM_BUCKETS - 1)
    return jnp.where(dist < max_exact, dist, large)


def _prompt_bias_kernel(rel_ref, bkt_ref, o_ref):
    bkt = bkt_ref[0]
    for h in range(o_ref.shape[1]):
        acc = jnp.full(bkt.shape, -jnp.inf, F32)
        for b in range(NUM_BUCKETS):
            acc = jnp.where(bkt == b, rel_ref[b, h], acc)
        o_ref[0, h] = acc


def _prompt_bias_tables(rel_bias):
    i = jnp.arange(BAND)[:, None]
    jj = jnp.arange(2 * BAND)[None, :]
    delta = BAND + i - jj
    valid = (delta >= 0) & (delta <= BAND)
    bkt = jnp.stack([jnp.where(valid, _rel_bucket(jnp.maximum(delta, 0) * d), -1)
                     for (_, d) in PATTERNS]).astype(jnp.int32)
    npat = len(PATTERNS)
    return pl.pallas_call(
        _prompt_bias_kernel,
        grid=(npat,),
        in_specs=[pl.BlockSpec(memory_space=pltpu.SMEM),
                  pl.BlockSpec((1, BAND, 2 * BAND), lambda p: (p, 0, 0))],
        out_specs=pl.BlockSpec((1, N_HEADS, BAND, 2 * BAND), lambda p: (p, 0, 0, 0)),
        out_shape=jax.ShapeDtypeStruct((npat, N_HEADS, BAND, 2 * BAND), F32),
        compiler_params=_params(("arbitrary",)),
        name="prompt_bias",
    )(rel_bias, bkt)


def _sample_bias_kernel(relrow_ref, bkt_ref, o_ref):
    bkt = bkt_ref[0]
    reps = bkt.shape[1] // LANES
    acc = jnp.full(bkt.shape, -jnp.inf, F32)
    for b in range(NUM_BUCKETS):
        acc = jnp.where(bkt == b, jnp.tile(relrow_ref[b], (1, reps)), acc)
    o_ref[0] = acc


def _sample_bias_tables(rel_bias, n_hist, t_new):
    n_keys = n_hist + LANES
    c = jnp.arange(n_keys)[None, :]
    t = jnp.arange(t_new)[:, None]
    key = jnp.where(c < n_hist, c, c - (LANES - t_new))
    real = (c < n_hist) | (c >= n_keys - t_new)
    dist = n_hist + t - key
    tabs = []
    for (w, d) in PATTERNS:
        valid = real & (dist >= 0) & (dist <= w) & (dist % d == 0)
        tabs.append(jnp.where(valid, _rel_bucket(jnp.maximum(dist, 0)), -1))
    bkt = jnp.stack(tabs).astype(jnp.int32)
    bkt = jnp.repeat(bkt, N_HEADS, axis=1)
    relrow = jnp.broadcast_to(jnp.tile(rel_bias, (1, t_new))[:, :, None],
                              (NUM_BUCKETS, t_new * N_HEADS, LANES))
    npat = len(PATTERNS)
    rows = t_new * N_HEADS
    return pl.pallas_call(
        _sample_bias_kernel,
        grid=(npat,),
        in_specs=[pl.BlockSpec((NUM_BUCKETS, rows, LANES), lambda p: (0, 0, 0)),
                  pl.BlockSpec((1, rows, n_keys), lambda p: (p, 0, 0))],
        out_specs=pl.BlockSpec((1, rows, n_keys), lambda p: (p, 0, 0)),
        out_shape=jax.ShapeDtypeStruct((npat, rows, n_keys), F32),
        compiler_params=_params(("arbitrary",)),
        name="sample_bias",
    )(relrow, bkt)


def _modulated_rms(x, g, scale, shift):
    y = x * lax.rsqrt(jnp.mean(x * x, axis=-1, keepdims=True) + EPS) * g
    return y * (1.0 + scale) + shift


def _mod_slice(mod, k, d):
    return mod[:, k * d:(k + 1) * d]


def _head_norm(t, g, headmean_ref):
    ms = _dot((t * t).astype(BF16), headmean_ref[...])
    return t * lax.rsqrt(ms + EPS) * g


def _in_proj(hb, w_in_ref, qg_ref, kg_ref, headmean_ref):
    aw = ATTN_W
    q = _head_norm(_dot(hb, w_in_ref[:, 0:aw]), qg_ref[...], headmean_ref)
    k = _head_norm(_dot(hb, w_in_ref[:, aw:2 * aw]), kg_ref[...], headmean_ref)
    v = _dot(hb, w_in_ref[:, 2 * aw:3 * aw])
    cw = (w_in_ref.shape[1] - 3 * aw) // 2
    ua = _dot(hb, w_in_ref[:, 3 * aw:3 * aw + cw])
    ub = _dot(hb, w_in_ref[:, 3 * aw + cw:3 * aw + 2 * cw])
    return q, k, v, ua * jax.nn.sigmoid(ub)


def _conv_post(dw, lng_ref, lnb_ref):
    mu = jnp.mean(dw, axis=-1, keepdims=True)
    xc = dw - mu
    var = jnp.mean(xc * xc, axis=-1, keepdims=True)
    y = xc * lax.rsqrt(var + EPS) * lng_ref[...] + lnb_ref[...]
    return (y * jax.nn.sigmoid(y)).astype(BF16)


def _mlp_tail(x, mix, mod, n2g_ref, w1_ref, w2_ref, a_ref, ff_chunk):
    d = x.shape[-1]
    x1 = x + _mod_slice(mod, 2, d) * mix
    h2 = _modulated_rms(x1, n2g_ref[...], _mod_slice(mod, 4, d), _mod_slice(mod, 3, d)).astype(BF16)
    for c in range(w1_ref.shape[1] // ff_chunk):
        a = jnp.maximum(_dot(h2, w1_ref[:, c * ff_chunk:(c + 1) * ff_chunk]), 0.0)
        a_ref[:, c * ff_chunk:(c + 1) * ff_chunk] = (a * a).astype(BF16)
    return x1 + _mod_slice(mod, 5, d) * _dot(a_ref[...], w2_ref[...])


def _slabs(x):
    return [x[:, s * LANES:(s + 1) * LANES] for s in range(x.shape[1] // LANES)]


def _regroup_store(x, nat, p4, out1_ref, out4_ref, out16_ref, which):
    tm = x.shape[0]
    d4 = PATTERNS[1][1]
    d16 = PATTERNS[2][1]
    sub = d16 // d4
    cols = slice(which * ATTN_W, (which + 1) * ATTN_W)
    out1_ref[0, :, cols] = x.astype(BF16)
    for s, xs in enumerate(_slabs(x)):
        nat[s] = xs
    for r4 in range(d4):
        parts = []
        for s in range(N_SLABS):
            part = nat[s, pl.ds(r4, tm // d4, stride=d4), :]
            p4[s, r4] = part
            parts.append(part)
        out4_ref[0, r4, :, cols] = jnp.concatenate(parts, axis=1).astype(BF16)
    for r16 in range(d16):
        r4, a = r16 % d4, r16 // d4
        parts = [p4[s, r4, pl.ds(a, tm // d16, stride=sub), :] for s in range(N_SLABS)]
        out16_ref[0, r16, :, cols] = jnp.concatenate(parts, axis=1).astype(BF16)


def _front_kernel(x_ref, mod_ref, n1g_ref, w_in_ref, qg_ref, kg_ref, headmean_ref,
                  qkv1_ref, qkv4_ref, qkv16_ref, glu_ref, kf_ref, vf_ref, nat, p4):
    d = x_ref.shape[2]
    mod = mod_ref[0]
    h = _modulated_rms(x_ref[0], n1g_ref[...], _mod_slice(mod, 1, d), _mod_slice(mod, 0, d))
    q, k, v, glu = _in_proj(h.astype(BF16), w_in_ref, qg_ref, kg_ref, headmean_ref)
    glu_ref[0] = glu
    kf_ref[0] = k
    vf_ref[0] = v
    for which, t in enumerate((q * (HEAD_DIM ** -0.5), k, v)):
        _regroup_store(t, nat, p4, qkv1_ref, qkv4_ref, qkv16_ref, which)


def _prompt_front(x, mod, lw, tm=1024):
    b, s, d = x.shape
    win = min(WINDOW_MAX, s)
    tail_start = (s - win) // tm
    cw = (lw["w_in"].shape[1] - 3 * ATTN_W) // 2
    d4, d16 = PATTERNS[1][1], PATTERNS[2][1]
    tok = lambda width: pl.BlockSpec((1, tm, width), lambda bi, i: (bi, i, 0))
    qkv_w = 3 * ATTN_W
    plane = lambda dil: pl.BlockSpec((1, dil, tm // dil, qkv_w), lambda bi, i: (bi, 0, i, 0))
    tail = pl.BlockSpec((1, tm, ATTN_W), lambda bi, i: (bi, jnp.maximum(i - tail_start, 0), 0))
    in_specs = [tok(d),
                pl.BlockSpec((1, 1, N_MOD * d), lambda bi, i: (bi, 0, 0)),
                _resident((1, d)), _resident(lw["w_in"].shape), _resident((1, ATTN_W)),
                _resident((1, ATTN_W)), _resident((ATTN_W, ATTN_W))]
    out_specs = [tok(qkv_w), plane(d4), plane(d16), tok(cw), tail, tail]
    plane_shape = lambda dil: jax.ShapeDtypeStruct((b, dil, s // dil, qkv_w), BF16)
    out_shape = ([jax.ShapeDtypeStruct((b, s, qkv_w), BF16), plane_shape(d4), plane_shape(d16)] + [
        jax.ShapeDtypeStruct((b, s, cw), F32),
        jax.ShapeDtypeStruct((b, win, ATTN_W), F32), jax.ShapeDtypeStruct((b, win, ATTN_W), F32)])
    return pl.pallas_call(
        _front_kernel,
        grid=(b, s // tm),
        in_specs=in_specs, out_specs=out_specs, out_shape=out_shape,
        scratch_shapes=[pltpu.VMEM((N_SLABS, tm, LANES), F32),
                        pltpu.VMEM((N_SLABS, d4, tm // d4, LANES), F32)],
        compiler_params=_params(("arbitrary", "arbitrary")),
        name="prompt_front",
    )(x, mod.reshape(b, 1, N_MOD * d), lw["norm1_g"], lw["w_in"], lw["q_norm_g"], lw["k_norm_g"],
      lw["headmean"])


def _attn_kernel(q_ref, kp_ref, kc_ref, vp_ref, vc_ref, bias_ref, o_ref, st_ref):
    i = pl.program_id(2)
    qb = q_ref.shape[2]
    lane = lax.broadcasted_iota(jnp.int32, (BAND, LANES), 1)
    low = lane < HEAD_DIM
    col = lax.broadcasted_iota(jnp.int32, (1, 2 * BAND), 1)
    first_mask = jnp.where((col < BAND) & (i == 0), -jnp.inf, 0.0)
    zero = jnp.zeros((), BF16)
    ones = jnp.ones((2 * BAND, LANES), BF16)

    def window(prev_ref, cur_ref, j, cols):
        if j == 0:
            return jnp.concatenate([prev_ref[0, 0, :, cols], cur_ref[0, 0, 0:BAND, cols]], axis=0)
        return cur_ref[0, 0, (j - 1) * BAND:(j + 1) * BAND, cols]

    for j in range(qb // BAND):
        rows = slice(j * BAND, (j + 1) * BAND)
        stat = jnp.zeros((BAND, LANES), F32)
        for p in range(N_SLABS):
            cols = slice(p * LANES, (p + 1) * LANES)
            q2 = q_ref[0, 0, rows, cols]
            k2 = window(kp_ref, kc_ref, j, cols)
            v2 = jnp.concatenate([window(vp_ref, vc_ref, j, cols), ones], axis=1)
            halves = []
            for e in range(LANES // HEAD_DIM):
                hd = p * (LANES // HEAD_DIM) + e
                qm = jnp.where(low if e == 0 else jnp.logical_not(low), q2, zero)
                s = lax.dot_general(qm, k2, _NT, preferred_element_type=F32) + bias_ref[hd]
                if j == 0:
                    s = s + first_mask
                m = jnp.max(s, axis=-1, keepdims=True)
                pv = _dot(jnp.exp(s - m).astype(BF16), v2)
                l = pv[:, LANES:]
                halves.append(pv[:, :LANES] * (1.0 / l))
                stat = jnp.where(lane == hd, m + jnp.log(l), stat)
            o_ref[0, 0, rows, cols] = jnp.where(low, halves[0], halves[1]).astype(BF16)
        st_ref[0, 0, rows, :] = stat


def _prompt_attention(qkv, bias, qb=1024):
    b, dil, u, _ = qkv.shape
    qb = min(qb, u)
    ratio = qb // BAND
    cur = lambda col: pl.BlockSpec((1, 1, qb, ATTN_W), lambda bi, r, i: (bi, r, i, col))
    prev = lambda col: pl.BlockSpec(
        (1, 1, BAND, ATTN_W), lambda bi, r, i: (bi, r, jnp.maximum(i * ratio - 1, 0), col))
    return pl.pallas_call(
        _attn_kernel,
        grid=(b, dil, u // qb),
        in_specs=[cur(0), prev(1), cur(1), prev(2), cur(2), _resident(bias.shape)],
        out_specs=[cur(0), pl.BlockSpec((1, 1, qb, LANES), lambda bi, r, i: (bi, r, i, 0))],
        out_shape=[jax.ShapeDtypeStruct((b, dil, u, ATTN_W), BF16),
                   jax.ShapeDtypeStruct((b, dil, u, LANES), F32)],
        compiler_params=_params(("arbitrary", "arbitrary", "arbitrary")),
        name=f"prompt_attn_d{dil}",
    )(qkv, qkv, qkv, qkv, qkv, bias)


def _back_kernel(x_ref, mod_ref, o1_ref, o4_ref, o16_ref, s1_ref, s4_ref, s16_ref,
                 glu_ref, halo_ref, cw_ref, cb_ref, lng_ref, lnb_ref, wpw_ref,
                 expand_ref, wo_ref, n2g_ref, w1_ref, w2_ref,
                 qbd_ref, kn_ref, vn_ref, ck_ref, cv_ref, tab_ref, hmask_ref,
                 y_ref, nk_ref, nv_ref, os_ref,
                 lnat, onat, gbuf, act, a_ref, kb, vb, *, row_chunk, ff_chunk, t_new, n_groups):
    tm = x_ref.shape[1]
    group = (pl.program_id(0) * pl.num_programs(1) + pl.program_id(1)) % n_groups
    _window_attend(qbd_ref[0, 0], lambda p: tab_ref[group, p],
                   kn_ref.at[0], vn_ref.at[0], ck_ref.at[0], cv_ref.at[0], hmask_ref,
                   nk_ref.at[0], nv_ref.at[0], os_ref.at[0, 0], kb, vb,
                   t_new=t_new, n_patterns=tab_ref.shape[1], row_chunk=row_chunk)

    def steps(first, count):
        return pl.ds(CONV_PITCH * first, count, stride=CONV_PITCH)

    n_cs = gbuf.shape[0]
    has_prev = (pl.program_id(1) > 0).astype(F32)
    for s, part in enumerate(_slabs(halo_ref[0] * has_prev)):
        gbuf[s, steps(0, HALO), :] = part
    for s, part in enumerate(_slabs(glu_ref[0])):
        gbuf[s, steps(HALO, tm), :] = part
    for which, (o_ref, s_ref) in enumerate(((o4_ref, s4_ref), (o16_ref, s16_ref))):
        dil = o_ref.shape[1]
        for r in range(dil):
            rows = pl.ds(r, tm // dil, stride=dil)
            lnat[which, rows, :] = s_ref[0, r]
            for s, part in enumerate(_slabs(o_ref[0, r])):
                onat[which, s, rows, :] = part.astype(F32)

    lead = HALO - CONV_HIST
    for c in range(tm // row_chunk):
        accs = [jnp.broadcast_to(b_s, (row_chunk, LANES)) for b_s in _slabs(cb_ref[...])]
        for j in range(CONV_K):
            for s in range(n_cs):
                accs[s] = accs[s] + (gbuf[s, steps(c * row_chunk + lead + j, row_chunk), :]
                                     * cw_ref[j:j + 1, s * LANES:(s + 1) * LANES])
        act[c * row_chunk:(c + 1) * row_chunk, :] = _conv_post(
            jnp.concatenate(accs, axis=1), lng_ref, lnb_ref)
    o_conv = _dot(act[...], wpw_ref[...]).astype(BF16)

    ls = [s1_ref[0, 0], lnat[0], lnat[1]]
    top = jnp.maximum(jnp.maximum(ls[0], ls[1]), ls[2])
    ps = [jnp.exp(t - top) for t in ls]
    inv = 1.0 / (ps[0] + ps[1] + ps[2])
    os = [o1_ref[0, 0].astype(F32)] + [
        jnp.concatenate([onat[which, s] for s in range(N_SLABS)], axis=1) for which in range(2)]
    o_attn = None
    for p_d, o_d in zip(ps, os):
        w = p_d * inv
        w_hi = w.astype(BF16)
        w_lo = (w - w_hi.astype(F32)).astype(BF16)
        w_full = _dot(jnp.concatenate([w_hi, w_lo], axis=1), expand_ref[...])
        term = w_full * o_d
        o_attn = term if o_attn is None else o_attn + term
    aw = o_attn.shape[1]
    mix = _dot(o_attn.astype(BF16), wo_ref[0:aw, :]) + _dot(o_conv, wo_ref[aw:, :])
    y_ref[0] = _mlp_tail(x_ref[0], mix, mod_ref[0], n2g_ref, w1_ref, w2_ref, a_ref, ff_chunk)


def _prompt_back(x, mod, outs, stats, glu, lw, sample, row_chunk=64, ff_chunk=1024):
    b, s, d = x.shape
    dff = lw["w_ff1"].shape[1]
    cw = glu.shape[2]
    qbd, k_new_t, v_new_t, cache_k_t, cache_v_t, tabs, hmask = sample
    nb, n_groups, rows_g, gw = qbd.shape
    n_hist = cache_k_t.shape[2]
    t_new = rows_g // (gw // HEAD_DIM)
    tm = (b * s) // (nb * n_groups)
    assert b * s == tm * nb * n_groups and s % tm == 0 and tm % row_chunk == 0
    n = s // tm
    unit = lambda bi, i: ((bi * n + i) // n_groups, (bi * n + i) % n_groups)
    hosted = lambda shape: pl.BlockSpec((1,) + shape, lambda bi, i: unit(bi, i) + (0,))
    hosted4 = lambda shape: pl.BlockSpec((1, 1) + shape, lambda bi, i: unit(bi, i) + (0, 0))
    tok = lambda width: pl.BlockSpec((1, tm, width), lambda bi, i: (bi, i, 0))
    plane = lambda a: pl.BlockSpec((1, a.shape[1], tm // a.shape[1], a.shape[3]),
                                   lambda bi, i: (bi, 0, i, 0))
    halo = pl.BlockSpec((1, HALO, cw), lambda bi, i: (bi, jnp.maximum(i * (tm // HALO) - 1, 0), 0))
    in_specs = ([tok(d), pl.BlockSpec((1, 1, N_MOD * d), lambda bi, i: (bi, 0, 0))]
                + [plane(a) for a in outs] + [plane(a) for a in stats]
                + [tok(cw), halo,
                   _resident((CONV_K, cw)), _resident((1, cw)), _resident((1, cw)), _resident((1, cw)),
                   _resident((cw, cw)),
                   _resident(lw["expand"].shape), _resident((d, d)), _resident((1, d)),
                   _resident((d, dff)), _resident((dff, d)),
                   hosted4((rows_g, gw)), hosted((gw, t_new)), hosted((gw, t_new)),
                   hosted((gw, n_hist)), hosted((gw, n_hist)),
                   _resident(tabs.shape), _resident(hmask.shape)])
    out_specs = [tok(d), hosted((gw, n_hist)), hosted((gw, n_hist)), hosted4((t_new, gw))]
    out_shape = [jax.ShapeDtypeStruct((b, s, d), F32),
                 jax.ShapeDtypeStruct(cache_k_t.shape, cache_k_t.dtype),
                 jax.ShapeDtypeStruct(cache_v_t.shape, cache_v_t.dtype),
                 jax.ShapeDtypeStruct((nb, n_groups, t_new, gw), F32)]
    return pl.pallas_call(
        functools.partial(_back_kernel, row_chunk=row_chunk, ff_chunk=ff_chunk,
                          t_new=t_new, n_groups=n_groups),
        grid=(b, n),
        in_specs=in_specs, out_specs=out_specs, out_shape=out_shape,
        scratch_shapes=[pltpu.VMEM((2, tm, LANES), F32),
                        pltpu.VMEM((2, N_SLABS, tm, LANES), F32),
                        pltpu.VMEM((cw // LANES, CONV_PITCH * (tm + HALO), LANES), F32),
                        pltpu.VMEM((tm, cw), BF16), pltpu.VMEM((tm, dff), BF16),
                        pltpu.VMEM((gw, n_hist + LANES), BF16), pltpu.VMEM((gw, n_hist + LANES), BF16)],
        compiler_params=_params(("arbitrary", "arbitrary")),
        name="prompt_back",
    )(x, mod.reshape(b, 1, N_MOD * d), *outs, *stats, glu, glu, lw["conv_dw"], lw["conv_dw_b"], lw["conv_ln_g"],
      lw["conv_ln_b"], lw["w_pw2"], lw["expand"], lw["w_o"], lw["norm2_g"], lw["w_ff1"], lw["w_ff2"],
      qbd, k_new_t, v_new_t, cache_k_t, cache_v_t, tabs, hmask)


def _sample_front_kernel(x_ref, mod_ref, hist_ref, n1g_ref, w_in_ref, qg_ref, kg_ref, headmean_ref,
                         cw_ref, cb_ref, lng_ref, lnb_ref, wpw_ref,
                         q_ref, k_ref, v_ref, glu_ref, oc_ref, hbuf, act, *, t_new):
    nb = mod_ref.shape[0]
    d = x_ref.shape[1]
    mod = mod_ref[...]
    for t in range(t_new):
        rows = slice(t * nb, (t + 1) * nb)
        h = _modulated_rms(x_ref[rows, :], n1g_ref[...], _mod_slice(mod, 1, d), _mod_slice(mod, 0, d))
        hbuf[rows, :] = h.astype(BF16)
    q, k, v, glu = _in_proj(hbuf[...], w_in_ref, qg_ref, kg_ref, headmean_ref)
    q_ref[...] = q
    k_ref[...] = k
    v_ref[...] = v
    glu_ref[...] = glu

    def slab(sidx):
        if sidx < CONV_HIST:
            return hist_ref[sidx]
        return glu_ref[(sidx - CONV_HIST) * nb:(sidx - CONV_HIST + 1) * nb, :]

    for t in range(t_new):
        acc = jnp.broadcast_to(cb_ref[...], (nb, glu.shape[1]))
        for j in range(CONV_K):
            acc = acc + slab(t + j) * cw_ref[j:j + 1, :]
        act[t * nb:(t + 1) * nb, :] = _conv_post(acc, lng_ref, lnb_ref)
    oc_ref[...] = _dot(act[...], wpw_ref[...]).astype(BF16)


def _sample_front(x_tm, mod, hist_tm, lw, t_new):
    rows, d = x_tm.shape
    cw = lw["conv_dw"].shape[1]
    full = lambda shape: pl.BlockSpec(shape, lambda i: (0,) * len(shape))
    args = (x_tm, mod, hist_tm, lw["norm1_g"], lw["w_in"], lw["q_norm_g"], lw["k_norm_g"],
            lw["headmean"], lw["conv_dw"], lw["conv_dw_b"], lw["conv_ln_g"], lw["conv_ln_b"],
            lw["w_pw2"])
    return pl.pallas_call(
        functools.partial(_sample_front_kernel, t_new=t_new),
        grid=(1,),
        in_specs=[full(a.shape) for a in args],
        out_specs=[full((rows, ATTN_W))] * 3 + [full((rows, cw)), full((rows, cw))],
        out_shape=[jax.ShapeDtypeStruct((rows, ATTN_W), F32)] * 3 + [
            jax.ShapeDtypeStruct((rows, cw), F32), jax.ShapeDtypeStruct((rows, cw), BF16)],
        scratch_shapes=[pltpu.VMEM((rows, d), BF16), pltpu.VMEM((rows, cw), BF16)],
        compiler_params=_params(("arbitrary",)),
        name="sample_front",
    )(*args)


def _window_attend(qbd, table, kn_ref, vn_ref, ck_ref, cv_ref, hmask_ref, nk_ref, nv_ref, o_ref,
                   kb, vb, *, t_new, n_patterns, row_chunk):
    width, n_hist = ck_ref.shape
    keep = n_hist - LANES
    lane = lax.broadcasted_iota(jnp.int32, (row_chunk, LANES), 1)
    is_new = lane >= LANES - t_new
    for src_ref, new_ref, dst_ref, buf in ((ck_ref, kn_ref, nk_ref, kb), (cv_ref, vn_ref, nv_ref, vb)):
        for c in range(0, width, row_chunk):
            rows = slice(c, c + row_chunk)
            old = src_ref[rows, :]
            new = jnp.concatenate([jnp.zeros((row_chunk, LANES - t_new), F32), new_ref[rows, :]], axis=1)
            buf[rows, 0:n_hist] = old.astype(BF16)
            buf[rows, n_hist:] = new.astype(BF16)
            shifted = pltpu.roll(old, n_hist - t_new, axis=1)
            dst_ref[rows, 0:keep] = shifted[:, 0:keep]
            dst_ref[rows, keep:] = jnp.where(is_new, new, shifted[:, keep:])

    s = _dot(qbd, kb[...])
    ms, ls, exs = [], [], []
    for p in range(n_patterns):
        sp = s + table(p)
        m = jnp.max(sp, axis=-1, keepdims=True)
        ex = jnp.exp(sp - m)
        ms.append(m)
        ls.append(jnp.sum(ex, axis=-1, keepdims=True))
        exs.append(ex.astype(BF16))
    pv = lax.dot_general(jnp.concatenate(exs, axis=0), vb[...], _NT, preferred_element_type=F32)
    nr = s.shape[0]
    accs = [pv[p * nr:(p + 1) * nr, :] for p in range(n_patterns)]
    top = functools.reduce(jnp.maximum, ms)
    ws = [jnp.exp(m - top) for m in ms]
    num = sum(w * a for w, a in zip(ws, accs))
    den = sum(w * l for w, l in zip(ws, ls))
    rows = num * (1.0 / den) * hmask_ref[...]
    heads = nr // t_new
    for t in range(t_new):
        o_ref[t:t + 1, :] = jnp.sum(rows[t * heads:(t + 1) * heads, :], axis=0, keepdims=True)


def _sample_back_kernel(x_ref, mod_ref, oa_ref, oc_ref, wo_ref, n2g_ref, w1_ref, w2_ref,
                        y_ref, a_ref, *, ff_chunk):
    aw = oa_ref.shape[1]
    mix = _dot(oa_ref[...].astype(BF16), wo_ref[0:aw, :]) + _dot(oc_ref[...], wo_ref[aw:, :])
    y_ref[...] = _mlp_tail(x_ref[...], mix, mod_ref[...], n2g_ref, w1_ref, w2_ref, a_ref, ff_chunk)


def _sample_back(x_tm, mod, o_attn_tm, o_conv_tm, lw, t_new, ff_chunk=512):
    rows, d = x_tm.shape
    nb = rows // t_new
    dff = lw["w_ff1"].shape[1]
    cw = o_conv_tm.shape[1]
    tok = lambda width: pl.BlockSpec((nb, width), lambda t: (t, 0))
    return pl.pallas_call(
        functools.partial(_sample_back_kernel, ff_chunk=ff_chunk),
        grid=(t_new,),
        in_specs=[tok(d), _resident(mod.shape), tok(ATTN_W), tok(cw),
                  _resident((d, d)), _resident((1, d)), _resident((d, dff)), _resident((dff, d))],
        out_specs=tok(d),
        out_shape=jax.ShapeDtypeStruct((rows, d), F32),
        scratch_shapes=[pltpu.VMEM((nb, dff), BF16)],
        compiler_params=_params(("arbitrary",)),
        name="sample_back",
    )(x_tm, mod, o_attn_tm, o_conv_tm, lw["w_o"], lw["norm2_g"], lw["w_ff1"], lw["w_ff2"])


def _layer_weights(l, norm1_g, norm2_g, w_in, q_norm_g, k_norm_g, conv_dw, conv_dw_b,
                   conv_ln_g, conv_ln_b, w_pw2, w_o, w_ff1, w_ff2):
    head_of = jnp.arange(ATTN_W) // HEAD_DIM
    headmean = (head_of[:, None] == head_of[None, :]).astype(F32) / HEAD_DIM
    expand = (jnp.arange(2 * LANES)[:, None] % LANES == head_of[None, :]).astype(BF16)
    row = lambda a: a[l].reshape(1, -1)
    return {
        "norm1_g": row(norm1_g), "norm2_g": row(norm2_g),
        "w_in": w_in[l].astype(BF16),
        "q_norm_g": jnp.tile(q_norm_g[l], N_HEADS).reshape(1, ATTN_W),
        "k_norm_g": jnp.tile(k_norm_g[l], N_HEADS).reshape(1, ATTN_W),
        "headmean": headmean.astype(BF16), "expand": expand,
        "conv_dw": conv_dw[l], "conv_dw_b": row(conv_dw_b),
        "conv_ln_g": row(conv_ln_g), "conv_ln_b": row(conv_ln_b),
        "w_pw2": w_pw2[l].astype(BF16), "w_o": w_o[l].astype(BF16),
        "w_ff1": w_ff1[l].astype(BF16), "w_ff2": w_ff2[l].astype(BF16),
    }


def _feature_major(cache):
    nb, n_hist = cache.shape[0], cache.shape[1]
    return cache.transpose(0, 2, 3, 1).reshape(nb, ATTN_W, n_hist)


def _position_major(cache_t):
    nb, _, n_hist = cache_t.shape
    return cache_t.reshape(nb, N_HEADS, HEAD_DIM, n_hist).transpose(0, 3, 1, 2)


def kernel(x_prompt, x_sample, cache_k, cache_v, state_conv, c_prompt, c_sample, rel_bias, norm1_g, norm2_g, w_ada, b_ada, w_in, q_norm_g, k_norm_g, conv_dw, conv_dw_b, conv_ln_g, conv_ln_b, w_pw2, w_o, w_ff1, w_ff2):
    depth = w_in.shape[0]
    b, s, d = x_prompt.shape
    nb, t_new, _ = x_sample.shape
    n_hist = cache_k.shape[2]
    assert s % WINDOW_MAX == 0 and n_hist % LANES == 0 and t_new <= LANES

    prompt_bias = _prompt_bias_tables(rel_bias)
    n_groups = SAMPLE_HEAD_GROUPS
    hpg = N_HEADS // n_groups
    gw = hpg * HEAD_DIM
    n_keys = n_hist + LANES
    sample_tabs = _sample_bias_tables(rel_bias, n_hist, t_new)
    sample_tabs = sample_tabs.reshape(len(PATTERNS), t_new, n_groups, hpg, n_keys).transpose(
        2, 0, 1, 3, 4).reshape(n_groups, len(PATTERNS), t_new * hpg, n_keys)
    hmask_g = (jnp.arange(hpg)[:, None] == (jnp.arange(gw) // HEAD_DIM)[None, :]).astype(F32)
    hmask = jnp.tile(hmask_g, (t_new, 1))

    xp = x_prompt
    xs_tm = x_sample.transpose(1, 0, 2).reshape(t_new * nb, d)
    c_all = jnp.concatenate([c_prompt, c_sample], axis=0)
    kp_l, vp_l, cp_l, ks_l, vs_l, cs_l = [], [], [], [], [], []
    for l in range(depth):
        lw = _layer_weights(l, norm1_g, norm2_g, w_in, q_norm_g, k_norm_g, conv_dw, conv_dw_b,
                            conv_ln_g, conv_ln_b, w_pw2, w_o, w_ff1, w_ff2)
        mod = _modulation(c_all, w_ada[l], b_ada[l])
        mod_p, mod_s = mod[:b], mod[b:]

        hist_tm = state_conv[l].transpose(1, 0, 2)
        q_s, k_s, v_s, glu_s, oc_s = _sample_front(xs_tm, mod_s, hist_tm, lw, t_new)
        to_bm = lambda a: a.reshape(t_new, nb, -1).transpose(1, 0, 2)
        q_bm = (to_bm(q_s) * (HEAD_DIM ** -0.5)).reshape(nb, t_new, n_groups, gw).transpose(0, 2, 1, 3)
        qbd = (q_bm[:, :, :, None, :] * hmask_g[None, None, None]).reshape(nb, n_groups, t_new * hpg, gw)
        new_t = lambda a: a.reshape(t_new, nb, ATTN_W).transpose(1, 2, 0)
        sample = (qbd.astype(BF16), new_t(k_s), new_t(v_s),
                  _feature_major(cache_k[l]), _feature_major(cache_v[l]), sample_tabs, hmask)

        qkv1, qkv4, qkv16, glu, k_tail, v_tail = _prompt_front(xp, mod_p, lw)
        outs, stats = [], []
        for pi, qkv in enumerate((qkv1.reshape(b, 1, s, -1), qkv4, qkv16)):
            o_d, st_d = _prompt_attention(qkv, prompt_bias[pi])
            outs.append(o_d)
            stats.append(st_d)
        xp, new_k_t, new_v_t, o_s = _prompt_back(xp, mod_p, outs, stats, glu, lw, sample)
        kp_l.append(k_tail.reshape(b, -1, N_HEADS, HEAD_DIM))
        vp_l.append(v_tail.reshape(b, -1, N_HEADS, HEAD_DIM))
        cp_l.append(glu[:, s - CONV_HIST:, :])

        o_s_tm = o_s.transpose(2, 0, 1, 3).reshape(t_new * nb, ATTN_W)
        xs_tm = _sample_back(xs_tm, mod_s, o_s_tm, oc_s, lw, t_new)
        ks_l.append(_position_major(new_k_t))
        vs_l.append(_position_major(new_v_t))
        cs_l.append(jnp.concatenate([state_conv[l].astype(F32), to_bm(glu_s)], axis=1)[:, -CONV_HIST:])

    xs = xs_tm.reshape(t_new, nb, d).transpose(1, 0, 2)
    return (xp, xs, jnp.stack(kp_l), jnp.stack(vp_l), jnp.stack(cp_l),
            jnp.stack(ks_l), jnp.stack(vs_l), jnp.stack(cs_l))
```
